```python
import math
import jax, jax.numpy as jnp
from jax import lax
import numpy as np

D_MODEL = 1024
BATCH = 8
SEQ = 2048
DEPTH = 2
DEC_BATCH = 128
DEC_SEQ = 8
PAST_LEN = 16384
PAGE_SIZE = 128

MIX_WIDTH = D_MODEL
A_WIDTH = MIX_WIDTH // 4
A_HEADS = 4
A_DK = A_WIDTH // A_HEADS
A_DV = A_DK
B_WIDTH = 3 * MIX_WIDTH // 8
B_HEADS = 6
B_DK = B_WIDTH // B_HEADS
B_DV = B_DK
C_WIDTH = MIX_WIDTH - A_WIDTH - B_WIDTH
C_HEADS = 4
C_DV = C_WIDTH // C_HEADS
C_DK = C_DV // 2
C_RANK = 16
C_TAU = 16.0
D_FF = ((8 * D_MODEL // 3 + 127) // 128) * 128
CHUNK = 16
ROPE_BASE = 10000.0
EPS = 1e-6

IN_WIDTHS = (A_HEADS * A_DK, A_HEADS * A_DK, A_HEADS * A_DV, A_HEADS * A_DV,
             B_HEADS * B_DK, B_HEADS * B_DK, B_HEADS * B_DV, B_HEADS * B_DV,
             C_HEADS * C_DK, C_HEADS * C_DK, C_HEADS * C_DV, C_HEADS * C_DV, C_RANK)
IN_WIDTH = sum(IN_WIDTHS)

kernel_name = "hymba_style_hgrn2_retnet_gla_macaron_step"


def _split_points():
    pts, acc = [], 0
    for w in IN_WIDTHS[:-1]:
        acc += w
        pts.append(acc)
    return pts


def rmsnorm(x, w):
    xf = x.astype(jnp.float32)
    y = xf * lax.rsqrt(jnp.mean(xf * xf, axis=-1, keepdims=True) + EPS)
    return (y * w.astype(jnp.float32)).astype(x.dtype)


def head_rmsnorm(o):
    return o * lax.rsqrt(jnp.mean(o * o, axis=-1, keepdims=True) + EPS)


def swiglu(h, w_in, w_out):
    gu = h @ w_in
    g, u = jnp.split(gu, 2, axis=-1)
    return (jax.nn.silu(g) * u) @ w_out


def rotary(x, pos):
    half = x.shape[-1] // 2
    inv_freq = ROPE_BASE ** (-jnp.arange(half, dtype=jnp.float32) / half)
    ang = pos[:, None] * inv_freq[None, :]
    cos = jnp.cos(ang)[None, :, None, :]
    sin = jnp.sin(ang)[None, :, None, :]
    x1, x2 = x[..., :half], x[..., half:]
    return jnp.concatenate([x1 * cos - x2 * sin, x1 * sin + x2 * cos], axis=-1)


def gated_linear_recurrence(q, k, v, log_decay, state0):
    Bsz, T, H, dk = q.shape
    dv = v.shape[-1]
    C = math.gcd(T, CHUNK)
    nc = T // C

    def to_blocks(a):
        return jnp.moveaxis(a.astype(jnp.float32).reshape(Bsz, nc, C, H, a.shape[-1]), 1, 0)

    qb, kb, vb, gb = to_blocks(q), to_blocks(k), to_blocks(v), to_blocks(log_decay)
    causal = jnp.tril(jnp.ones((C, C), dtype=bool))[None, :, :, None, None]

    def step(S, inp):
        q_, k_, v_, g_ = inp
        b = jnp.cumsum(g_, axis=1)
        rel = jnp.exp(jnp.where(causal, b[:, :, None] - b[:, None, :], -jnp.inf))
        scores = jnp.einsum('bthd,bshd,btshd->bhts', q_, k_, rel)
        o = (jnp.einsum('bhts,bshv->bthv', scores, v_)
             + jnp.einsum('bthd,bhdv->bthv', q_ * jnp.exp(b), S))
        b_last = b[:, -1]
        S = (jnp.exp(b_last)[..., None] * S
             + jnp.einsum('bshd,bshv->bhdv', k_ * jnp.exp(b_last[:, None] - b), v_))
        return S, o

    S_final, ob = lax.scan(step, state0.astype(jnp.float32), (qb, kb, vb, gb))
    o = jnp.moveaxis(ob, 0, 1).reshape(Bsz, T, H, dv)
    return o, S_final


def mixing_block(h, pos, lb, w_in, w_out, hgrn_norm, gla_w2, gla_b, gla_norm, sA, sB, sC):
    Bsz, T, _ = h.shape
    f32 = jnp.float32
    proj = h @ w_in
    (aq, af, ai, ag, bq, bk, bv, bg, cq, ck, cv, cg, clr) = jnp.split(proj, _split_points(), axis=-1)

    def heads(a, H):
        return a.astype(f32).reshape(Bsz, T, H, -1)

    lb = lb.astype(f32)
    af32 = af.astype(f32)
    log_f = jnp.logaddexp(jnp.log(lb), jnp.log1p(-lb) + jax.nn.log_sigmoid(af32))
    k_a = (1.0 - lb) * jax.nn.sigmoid(-af32)
    q_a = jax.nn.silu(aq.astype(f32))
    if sA is None:
        sA = jnp.zeros((Bsz, A_HEADS, A_DK, A_DV), f32)
    oA, sA_new = gated_linear_recurrence(heads(q_a, A_HEADS), heads(k_a, A_HEADS), heads(ai, A_HEADS),
                                         heads(log_f, A_HEADS), sA)
    yA = head_rmsnorm(oA) * hgrn_norm.astype(f32) * jax.nn.silu(heads(ag, A_HEADS))

    q_b = rotary(heads(bq, B_HEADS), pos)
    k_b = rotary(heads(bk, B_HEADS), pos) * (B_DK ** -0.5)
    log_gamma = jnp.log(1.0 - 2.0 ** (-5.0 - jnp.arange(B_HEADS, dtype=f32)))
    g_b = jnp.broadcast_to(log_gamma[None, None, :, None], (Bsz, T, B_HEADS, B_DK))
    if sB is None:
        sB = jnp.zeros((Bsz, B_HEADS, B_DK, B_DV), f32)
    oB, sB_new = gated_linear_recurrence(q_b, k_b, heads(bv, B_HEADS), g_b, sB)
    yB = head_rmsnorm(oB) * jax.nn.silu(heads(bg, B_HEADS))

    gk = clr @ gla_w2 + gla_b
    log_a = jax.nn.log_sigmoid(gk.astype(f32)) / C_TAU
    q_c = heads(cq, C_HEADS) * (C_DK ** -0.5)
    if sC is None:
        sC = jnp.zeros((Bsz, C_HEADS, C_DK, C_DV), f32)
    oC, sC_new = gated_linear_recurrence(q_c, heads(ck, C_HEADS), heads(cv, C_HEADS),
                                         heads(log_a, C_HEADS), sC)
    yC = head_rmsnorm(oC) * gla_norm.astype(f32) * jax.nn.silu(heads(cg, C_HEADS))

    y = jnp.concatenate([yA.reshape(Bsz, T, -1), yB.reshape(Bsz, T, -1), yC.reshape(Bsz, T, -1)],
                        axis=-1).astype(h.dtype)
    return y @ w_out, sA_new, sB_new, sC_new


def trunk(x, pos, states, ffn1_norm, ffn1_w_in, ffn1_w_out, mix_norm, w_in, hgrn_lb_logits, hgrn_norm,
          gla_w2, gla_b, gla_norm, w_out, ffn2_norm, ffn2_w_in, ffn2_w_out, final_norm):
    p = jax.nn.softmax(hgrn_lb_logits.astype(jnp.float32), axis=0)
    cs = jnp.cumsum(p, axis=0)
    lbs = cs - cs[0:1]
    newA, newB, newC = [], [], []
    for l in range(DEPTH):
        x = x + 0.5 * swiglu(rmsnorm(x, ffn1_norm[l]), ffn1_w_in[l], ffn1_w_out[l])
        sA = None if states is None else states[0][l]
        sB = None if states is None else states[1][l]
        sC = None if states is None else states[2][l]
        y, sA, sB, sC = mixing_block(rmsnorm(x, mix_norm[l]), pos, lbs[l], w_in[l], w_out[l],
                                     hgrn_norm[l], gla_w2[l], gla_b[l], gla_norm[l], sA, sB, sC)
        x = x + y
        x = x + 0.5 * swiglu(rmsnorm(x, ffn2_norm[l]), ffn2_w_in[l], ffn2_w_out[l])
        newA.append(sA)
        newB.append(sB)
        newC.append(sC)
    return rmsnorm(x, final_norm), jnp.stack(newA), jnp.stack(newB), jnp.stack(newC)


def setup_inputs(seed: int = 0) -> dict:
    key = jax.random.key(seed)
    ks = jax.random.split(key, 24)
    f32 = jnp.float32

    def nrm(k, shape, scale):
        return jax.random.normal(k, shape, f32) * scale

    def gain(k, shape):
        return 1.0 + 0.01 * jax.random.normal(k, shape, f32)

    return {
        "x_prompt": nrm(ks[0], (BATCH, SEQ, D_MODEL), 1.0),
        "x_sample": nrm(ks[1], (DEC_BATCH, DEC_SEQ, D_MODEL), 1.0),
        "state_hgrn": nrm(ks[2], (DEPTH, DEC_BATCH, A_HEADS, A_DK, A_DV), 0.5),
        "state_ret": nrm(ks[3], (DEPTH, DEC_BATCH, B_HEADS, B_DK, B_DV), 0.5),
        "state_gla": nrm(ks[4], (DEPTH, DEC_BATCH, C_HEADS, C_DK, C_DV), 0.5),
        "ffn1_norm": gain(ks[5], (DEPTH, D_MODEL)),
        "ffn1_w_in": nrm(ks[6], (DEPTH, D_MODEL, 2 * D_FF), D_MODEL ** -0.5),
        "ffn1_w_out": nrm(ks[7], (DEPTH, D_FF, D_MODEL), D_FF ** -0.5),
        "mix_norm": gain(ks[8], (DEPTH, D_MODEL)),
        "w_in": nrm(ks[9], (DEPTH, D_MODEL, IN_WIDTH), D_MODEL ** -0.5),
        "hgrn_lb_logits": nrm(ks[10], (DEPTH, A_HEADS * A_DK), 0.1),
        "hgrn_norm": gain(ks[11], (DEPTH, A_DV)),
        "gla_w2": nrm(ks[12], (DEPTH, C_RANK, C_HEADS * C_DK), C_RANK ** -0.5),
        "gla_b": nrm(ks[13], (DEPTH, C_HEADS * C_DK), 0.1),
        "gla_norm": gain(ks[14], (DEPTH, C_DV)),
        "w_out": nrm(ks[15], (DEPTH, MIX_WIDTH, D_MODEL), MIX_WIDTH ** -0.5),
        "ffn2_norm": gain(ks[16], (DEPTH, D_MODEL)),
        "ffn2_w_in": nrm(ks[17], (DEPTH, D_MODEL, 2 * D_FF), D_MODEL ** -0.5),
        "ffn2_w_out": nrm(ks[18], (DEPTH, D_FF, D_MODEL), D_FF ** -0.5),
        "final_norm": gain(ks[19], (D_MODEL,)),
    }


def reference(x_prompt, x_sample, state_hgrn, state_ret, state_gla, ffn1_norm, ffn1_w_in, ffn1_w_out,
              mix_norm, w_in, hgrn_lb_logits, hgrn_norm, gla_w2, gla_b, gla_norm, w_out, ffn2_norm,
              ffn2_w_in, ffn2_w_out, final_norm):
    pos_prompt = jnp.arange(SEQ, dtype=jnp.float32)
    pos_sample = PAST_LEN + jnp.arange(DEC_SEQ, dtype=jnp.float32)
    y_prompt, hgrn_p, ret_p, gla_p = trunk(
        x_prompt, pos_prompt, None, ffn1_norm, ffn1_w_in, ffn1_w_out, mix_norm, w_in, hgrn_lb_logits,
        hgrn_norm, gla_w2, gla_b, gla_norm, w_out, ffn2_norm, ffn2_w_in, ffn2_w_out, final_norm)
    y_sample, hgrn_s, ret_s, gla_s = trunk(
        x_sample, pos_sample, (state_hgrn, state_ret, state_gla), ffn1_norm, ffn1_w_in, ffn1_w_out,
        mix_norm, w_in, hgrn_lb_logits, hgrn_norm, gla_w2, gla_b, gla_norm, w_out, ffn2_norm, ffn2_w_in,
        ffn2_w_out, final_norm)
    return (y_prompt, y_sample, hgrn_p, ret_p, gla_p, hgrn_s, ret_s, gla_s)
```

```python
import functools
import math

import jax
import jax.numpy as jnp
from jax import lax
from jax.experimental import pallas as pl
from jax.experimental.pallas import tpu as pltpu

F32, BF16 = jnp.float32, jnp.bfloat16

D_MODEL = 1024
DEPTH = 2
A_HEADS, A_DK, A_DV = 4, 64, 64
B_HEADS, B_DK, B_DV = 6, 64, 64
C_HEADS, C_DK, C_DV = 4, 48, 96
C_RANK = 16
C_TAU = 16.0
D_FF = 2816
ROPE_BASE = 10000.0
EPS = 1e-6
PAST_LEN = 16384

A_W = A_HEADS * A_DK
B_W = B_HEADS * B_DK
C_QW = C_HEADS * C_DK
C_VW = C_HEADS * C_DV
LANES = 128
C_QP = 256
C_RP = LANES

O_AQ, O_AF, O_AI, O_AG = 0, 256, 512, 768
O_BQ, O_BK, O_BV, O_BG = 1024, 1408, 1792, 2176
O_CQ, O_CK, O_CV, O_CG, O_CR = 2560, 2816, 3072, 3456, 3840
NP = O_CR + C_RP

ROWS = 128
FF_CHUNK = 256
TM = 256
VMEM_LIMIT = 56 * 1024 * 1024


def _dot(a, b):
    return jnp.dot(a, b, preferred_element_type=F32)


def _dot_nt(a, b):
    return lax.dot_general(a, b, (((1,), (1,)), ((), ())), preferred_element_type=F32)


def _dot_tn(a, b):
    return lax.dot_general(a, b, (((0,), (0,)), ((), ())), preferred_element_type=F32)


def _iota(shape, dim):
    return lax.broadcasted_iota(jnp.int32, shape, dim)


def _split3(x):
    hi = x.astype(BF16)
    r = x - hi.astype(F32)
    mid = r.astype(BF16)
    lo = (r - mid.astype(F32)).astype(BF16)
    return hi, mid, lo


def _dot01(mat01, x):
    hi, mid, lo = _split3(x)
    return _dot(mat01, hi) + _dot(mat01, mid) + _dot(mat01, lo)


def _rms(x, w):
    ms = jnp.mean(x * x, axis=-1, keepdims=True)
    return x * lax.rsqrt(ms + EPS) * w


def _silu(x):
    return x * jax.nn.sigmoid(x)


def _swiglu(h_bf, win_ref, wout_ref, act_ref):
    for c in range(D_FF // FF_CHUNK):
        lo = c * FF_CHUNK
        g = _dot(h_bf, win_ref[:, lo:lo + FF_CHUNK])
        u = _dot(h_bf, win_ref[:, D_FF + lo:D_FF + lo + FF_CHUNK])
        act_ref[:, lo:lo + FF_CHUNK] = (_silu(g) * u).astype(BF16)
    return _dot(act_ref[...], wout_ref[...])


def _ffn_proj_kernel(x_ref, n1_ref, win_ref, wout_ref, nm_ref, wmix_ref, x1_ref, proj_ref, act_ref):
    x = x_ref[...]
    h = _rms(x, n1_ref[...]).astype(BF16)
    x1 = x + 0.5 * _swiglu(h, win_ref, wout_ref, act_ref)
    x1_ref[...] = x1
    hm = _rms(x1, nm_ref[...]).astype(BF16)
    proj_ref[...] = _dot(hm, wmix_ref[...])


def _out_ffn_kernel(x_ref, y_ref, wo_ref, n2_ref, win_ref, wout_ref, nf_ref, o_ref, act_ref, *, final):
    x2 = x_ref[...] + _dot(y_ref[...], wo_ref[...])
    h = _rms(x2, n2_ref[...]).astype(BF16)
    x3 = x2 + 0.5 * _swiglu(h, win_ref, wout_ref, act_ref)
    if final:
        x3 = _rms(x3, nf_ref[...])
    o_ref[...] = x3


def _resident(shape):
    return pl.BlockSpec(shape, lambda i: (0,) * len(shape), pipeline_mode=pl.Buffered(1))


def _ffn_proj(x, n1, win, wout, nm, wmix):
    m = x.shape[0]
    row = lambda w: pl.BlockSpec((TM, w), lambda i: (i, 0))
    return pl.pallas_call(
        _ffn_proj_kernel,
        grid=(m // TM,),
        in_specs=[row(D_MODEL), _resident((1, D_MODEL)), _resident((D_MODEL, 2 * D_FF)),
                  _resident((D_FF, D_MODEL)), _resident((1, D_MODEL)), _resident((D_MODEL, NP))],
        out_specs=[row(D_MODEL), row(NP)],
        out_shape=[jax.ShapeDtypeStruct((m, D_MODEL), F32), jax.ShapeDtypeStruct((m, NP), F32)],
        scratch_shapes=[pltpu.VMEM((TM, D_FF), BF16)],
        compiler_params=pltpu.CompilerParams(dimension_semantics=("parallel",), vmem_limit_bytes=VMEM_LIMIT),
        name="ffn1_proj",
    )(x, n1, win, wout, nm, wmix)


def _out_ffn(x, y, wo, n2, win, wout, nf, final):
    m = x.shape[0]
    row = lambda w: pl.BlockSpec((TM, w), lambda i: (i, 0))
    return pl.pallas_call(
        functools.partial(_out_ffn_kernel, final=final),
        grid=(m // TM,),
        in_specs=[row(D_MODEL), row(D_MODEL), _resident((D_MODEL, D_MODEL)), _resident((1, D_MODEL)),
                  _resident((D_MODEL, 2 * D_FF)), _resident((D_FF, D_MODEL)), _resident((1, D_MODEL))],
        out_specs=row(D_MODEL),
        out_shape=jax.ShapeDtypeStruct((m, D_MODEL), F32),
        scratch_shapes=[pltpu.VMEM((TM, D_FF), BF16)],
        compiler_params=pltpu.CompilerParams(dimension_semantics=("parallel",), vmem_limit_bytes=VMEM_LIMIT),
        name="outproj_ffn2",
    )(x, y, wo, n2, win, wout, nf)


def _log2(n):
    assert n & (n - 1) == 0
    return n.bit_length() - 1


def _head_mask(width, dh, h):
    lane = _iota((1, width), 1)
    return (lane >= dh * h) & (lane < dh * (h + 1))


def _head_grid_mask(rows, cols, dr, dc, n_heads):
    r = _iota((rows, cols), 0)
    c = _iota((rows, cols), 1)
    m = None
    for h in range(n_heads):
        mh = (r >= dr * h) & (r < dr * (h + 1)) & (c >= dc * h) & (c < dc * (h + 1))
        m = mh if m is None else (m | mh)
    return m


def _seg_mean_sq(o, dh, n_heads):
    w = o.shape[1]
    ones = jnp.where(_head_grid_mask(w, w, dh, dh, n_heads), 1.0, 0.0).astype(BF16)
    sq = o * o
    hi = sq.astype(BF16)
    lo = (sq - hi.astype(F32)).astype(BF16)
    return (_dot(hi, ones) + _dot(lo, ones)) * (1.0 / dh)


def _head_norm(o, dh, n_heads):
    return o * lax.rsqrt(_seg_mean_sq(o, dh, n_heads) + EPS)


def _seq_cumsum(g, seq_len):
    rows = _iota((ROWS, ROWS), 0)
    cols = _iota((ROWS, ROWS), 1)
    sh = _log2(seq_len)
    tril = jnp.where(((rows >> sh) == (cols >> sh)) & (cols <= rows), 1.0, 0.0).astype(BF16)
    return _dot01(tril, g)


def _block_row(b, size, idx):
    c = b.shape[1]
    if size >= 8:
        b3 = b.reshape(ROWS // size, size, c)
        return jnp.broadcast_to(b3[:, idx:idx + 1, :], b3.shape).reshape(ROWS, c)
    b8 = b.reshape(ROWS // 8, 8, c)
    sub = _iota(b8.shape, 1)
    ref = b8[:, idx:idx + 1, :]
    for blk in range(1, 8 // size):
        lo = blk * size
        ref = jnp.where(sub >= lo, b8[:, lo + idx:lo + idx + 1, :], ref)
    return jnp.broadcast_to(ref, b8.shape).reshape(ROWS, c)


def _stack_heads(x, dh, n_heads, extra_mask=None):
    parts = []
    for h in range(n_heads):
        m = _head_mask(x.shape[1], dh, h)
        if extra_mask is not None:
            m = m & extra_mask
        parts.append(jnp.where(m, x, 0.0))
    return jnp.concatenate(parts, axis=0).astype(BF16)


def _apply_scores(p, v, dv, n_heads):
    pcat = jnp.concatenate([p[h].astype(BF16) for h in range(n_heads)], axis=1)
    return _dot(pcat, _stack_heads(v, dv, n_heads))


def _gated_intra(q, k, v, b, dk, dv, n_heads, seq_len):
    rows = _iota((ROWS, ROWS), 0)
    cols = _iota((ROWS, ROWS), 1)
    rowc = _iota((ROWS, q.shape[1]), 0)
    k_bf = k.astype(BF16)
    sc = _dot_nt(_stack_heads(q, dk, n_heads), k_bf).reshape(n_heads, ROWS, ROWS)
    total = jnp.where((rows == cols)[None], sc, 0.0)
    m = 1
    while m < seq_len:
        ref = _block_row(b, 2 * m, m - 1)
        up = (rowc & (2 * m - 1)) >= m
        e = jnp.exp(jnp.where(up, b - ref, ref - b))
        ql = _stack_heads(q * e, dk, n_heads, extra_mask=up)
        kl = jnp.where(up, 0.0, k * e).astype(BF16)
        sc = _dot_nt(ql, kl).reshape(n_heads, ROWS, ROWS)
        sh = _log2(2 * m)
        total = total + jnp.where(((rows >> sh) == (cols >> sh))[None], sc, 0.0)
        m *= 2
    return _apply_scores(total, v, dv, n_heads)


def _column(row_vec):
    c = row_vec.shape[1]
    return jnp.transpose(jnp.broadcast_to(row_vec, (8, c)))[:, 0:1]


def _gated_prompt(q, k, v, g, dk, dv, n_heads, s_ref):
    b = _seq_cumsum(g, ROWS)
    o = _gated_intra(q, k, v, b, dk, dv, n_heads, ROWS)
    s0 = s_ref[...]
    o = o + _dot((q * jnp.exp(b)).astype(BF16), s0.astype(BF16))
    b_last = b[ROWS - 1:ROWS, :]
    khat = (k * jnp.exp(b_last - b)).astype(BF16)
    ds = _dot_tn(khat, v.astype(BF16))
    keep = _head_grid_mask(q.shape[1], v.shape[1], dk, dv, n_heads)
    s_ref[...] = s0 * _column(jnp.exp(b_last)) + jnp.where(keep, ds, 0.0)
    return o


def _gated_sample(q, k, v, g, dk, dv, n_heads, seq_len, s_ref, snew_ref):
    n_seq = ROWS // seq_len
    b = _seq_cumsum(g, seq_len)
    o = _gated_intra(q, k, v, b, dk, dv, n_heads, seq_len)
    qhat = (q * jnp.exp(b)).astype(BF16)
    b_last = _block_row(b, seq_len, seq_len - 1)
    khat = (k * jnp.exp(b_last - b)).astype(BF16)
    decay_t = jnp.exp(jnp.transpose(b))
    v_bf = v.astype(BF16)
    o_inter = []
    for h in range(n_heads):
        s0 = s_ref[:, h]
        qh = qhat[:, dk * h:dk * (h + 1)].reshape(n_seq, seq_len, dk)
        oi = jnp.einsum('bqd,bdv->bqv', qh, s0.astype(BF16), preferred_element_type=F32)
        o_inter.append(oi.reshape(ROWS, dv))
        kh = khat[:, dk * h:dk * (h + 1)].reshape(n_seq, seq_len, dk)
        vh = v_bf[:, dv * h:dv * (h + 1)].reshape(n_seq, seq_len, dv)
        ds = jnp.einsum('bsd,bsv->bdv', kh, vh, preferred_element_type=F32)
        for s in range(n_seq):
            last = seq_len * s + seq_len - 1
            col = decay_t[dk * h:dk * (h + 1), last:last + 1]
            snew_ref[s, h] = col * s0[s] + ds[s]
    pad = v.shape[1] - n_heads * dv
    if pad:
        o_inter.append(jnp.zeros((ROWS, pad), F32))
    return o + jnp.concatenate(o_inter, axis=1)


_LOG_GAMMA = tuple(math.log(1.0 - 2.0 ** (-5.0 - h)) for h in range(B_HEADS))


def _per_head_lanes(width, dh, values):
    lane = _iota((1, width), 1)
    out = jnp.zeros((1, width), F32)
    for h, val in enumerate(values):
        out = jnp.where((lane >= dh * h) & (lane < dh * (h + 1)), val, out)
    return out


def _rotary(x, cos, sin_signed):
    w = x.shape[1]
    lane = _iota((1, w), 1)
    first_half = (lane & (B_DK - 1)) < (B_DK // 2)
    partner = jnp.where(first_half, pltpu.roll(x, w - B_DK // 2, 1), pltpu.roll(x, B_DK // 2, 1))
    return x * cos + partner * sin_signed


def _retention_intra(q, k, v, seq_len):
    rows = _iota((ROWS, ROWS), 0)
    cols = _iota((ROWS, ROWS), 1)
    sh = _log2(seq_len)
    causal = ((rows >> sh) == (cols >> sh)) & (cols <= rows)
    dist = (rows - cols).astype(F32)
    sc = _dot_nt(_stack_heads(q, B_DK, B_HEADS), k.astype(BF16)).reshape(B_HEADS, ROWS, ROWS)
    gam = jnp.stack([jnp.where(causal, jnp.exp(dist * lg), 0.0) for lg in _LOG_GAMMA])
    return _apply_scores(sc * gam, v, B_DV, B_HEADS)


def _retention_scaled(q, k, seq_len):
    tau = (_iota((ROWS, 1), 0) & (seq_len - 1)).astype(F32)
    lg = _per_head_lanes(B_W, B_DK, _LOG_GAMMA)
    qhat = (q * jnp.exp(lg * (tau + 1.0))).astype(BF16)
    khat = (k * jnp.exp(lg * (seq_len - 1.0 - tau))).astype(BF16)
    return qhat, khat


def _retention_prompt(q, k, v, s_ref):
    o = _retention_intra(q, k, v, ROWS)
    qhat, khat = _retention_scaled(q, k, ROWS)
    s0 = s_ref[...]
    o = o + _dot(qhat, s0.astype(BF16))
    ds = _dot_tn(khat, v.astype(BF16))
    keep = _head_grid_mask(B_W, B_W, B_DK, B_DV, B_HEADS)
    decay = _column(_per_head_lanes(B_W, B_DK, [math.exp(ROWS * lg) for lg in _LOG_GAMMA]))
    s_ref[...] = s0 * decay + jnp.where(keep, ds, 0.0)
    return o


def _retention_sample(q, k, v, seq_len, s_ref, snew_ref):
    n_seq = ROWS // seq_len
    o = _retention_intra(q, k, v, seq_len)
    qhat, khat = _retention_scaled(q, k, seq_len)
    v_bf = v.astype(BF16)
    o_inter = []
    for h in range(B_HEADS):
        s0 = s_ref[:, h]
        qh = qhat[:, B_DK * h:B_DK * (h + 1)].reshape(n_seq, seq_len, B_DK)
        oi = jnp.einsum('bqd,bdv->bqv', qh, s0.astype(BF16), preferred_element_type=F32)
        o_inter.append(oi.reshape(ROWS, B_DV))
        kh = khat[:, B_DK * h:B_DK * (h + 1)].reshape(n_seq, seq_len, B_DK)
        vh = v_bf[:, B_DV * h:B_DV * (h + 1)].reshape(n_seq, seq_len, B_DV)
        ds = jnp.einsum('bsd,bsv->bdv', kh, vh, preferred_element_type=F32)
        snew_ref[:, h] = s0 * math.exp(seq_len * _LOG_GAMMA[h]) + ds
    return o + jnp.concatenate(o_inter, axis=1)


def _lower_bound(lbl_ref, layer):
    rows = [lbl_ref[i:i + 1, :] for i in range(DEPTH)]
    mx = functools.reduce(jnp.maximum, rows)
    ex = [jnp.exp(r - mx) for r in rows]
    tot = functools.reduce(lambda a, c: a + c, ex)
    acc = functools.reduce(lambda a, c: a + c, ex[:layer + 1]) / tot
    return acc - ex[0] / tot


def _mix_tile(proj_ref, cos_ref, sin_ref, lbl_ref, hn_ref, w2_ref, gb_ref, gn_ref, layer, gated, retention):
    seg = lambda off, w: proj_ref[:, off:off + w]
    af = seg(O_AF, A_W)
    if layer == 0:
        log_f = jax.nn.log_sigmoid(af)
        k_a = jax.nn.sigmoid(-af)
    else:
        lb = _lower_bound(lbl_ref, layer)
        log_f = jnp.logaddexp(jnp.log(lb), jnp.log1p(-lb) + jax.nn.log_sigmoid(af))
        k_a = (1.0 - lb) * jax.nn.sigmoid(-af)
    o_a = gated("a", _silu(seg(O_AQ, A_W)), k_a, seg(O_AI, A_W), log_f, A_DK, A_DV, A_HEADS)
    y_a = _head_norm(o_a, A_DV, A_HEADS) * hn_ref[...] * _silu(seg(O_AG, A_W))
    q_b = _rotary(seg(O_BQ, B_W), cos_ref[...], sin_ref[...])
    k_b = _rotary(seg(O_BK, B_W), cos_ref[...], sin_ref[...]) * (B_DK ** -0.5)
    o_b = retention(q_b, k_b, seg(O_BV, B_W))
    y_b = _head_norm(o_b, B_DV, B_HEADS) * _silu(seg(O_BG, B_W))
    gk = _dot(seg(O_CR, C_RP).astype(BF16), w2_ref[...]) + gb_ref[...]
    log_a = jax.nn.log_sigmoid(gk) * (1.0 / C_TAU)
    o_c = gated("c", seg(O_CQ, C_QP) * (C_DK ** -0.5), seg(O_CK, C_QP), seg(O_CV, C_VW), log_a,
                C_DK, C_DV, C_HEADS)
    y_c = _head_norm(o_c, C_DV, C_HEADS) * gn_ref[...] * _silu(seg(O_CG, C_VW))
    return jnp.concatenate([y_a, y_b, y_c], axis=1).astype(BF16)


def _mix_prompt_kernel(proj_ref, cos_ref, sin_ref, lbl_ref, hn_ref, w2_ref, gb_ref, gn_ref,
                       y_ref, sa_ref, sb_ref, sc_ref, sa_acc, sb_acc, sc_acc, *, layer):
    t = pl.program_id(1)

    @pl.when(t == 0)
    def _():
        sa_acc[...] = jnp.zeros_like(sa_acc)
        sb_acc[...] = jnp.zeros_like(sb_acc)
        sc_acc[...] = jnp.zeros_like(sc_acc)

    def gated(which, q, k, v, g, dk, dv, n_heads):
        return _gated_prompt(q, k, v, g, dk, dv, n_heads, sa_acc if which == "a" else sc_acc)

    def retention(q, k, v):
        return _retention_prompt(q, k, v, sb_acc)

    y_ref[...] = _mix_tile(proj_ref, cos_ref, sin_ref, lbl_ref, hn_ref, w2_ref, gb_ref, gn_ref, layer,
                           gated, retention)

    @pl.when(t == pl.num_programs(1) - 1)
    def _():
        for h in range(A_HEADS):
            sa_ref[h] = sa_acc[A_DK * h:A_DK * (h + 1), A_DV * h:A_DV * (h + 1)]
        for h in range(B_HEADS):
            sb_ref[h] = sb_acc[B_DK * h:B_DK * (h + 1), B_DV * h:B_DV * (h + 1)]
        for h in range(C_HEADS):
            sc_ref[h] = sc_acc[C_DK * h:C_DK * (h + 1), C_DV * h:C_DV * (h + 1)]


def _mix_sample_kernel(proj_ref, cos_ref, sin_ref, lbl_ref, hn_ref, w2_ref, gb_ref, gn_ref,
                       sa_ref, sb_ref, sc_ref, y_ref, sa_new, sb_new, sc_new, *, layer, seq_len):
    def gated(which, q, k, v, g, dk, dv, n_heads):
        s_ref, snew_ref = (sa_ref, sa_new) if which == "a" else (sc_ref, sc_new)
        return _gated_sample(q, k, v, g, dk, dv, n_heads, seq_len, s_ref, snew_ref)

    def retention(q, k, v):
        return _retention_sample(q, k, v, seq_len, sb_ref, sb_new)

    y_ref[...] = _mix_tile(proj_ref, cos_ref, sin_ref, lbl_ref, hn_ref, w2_ref, gb_ref, gn_ref, layer,
                           gated, retention)


def _const(shape):
    return pl.BlockSpec(shape, lambda *_: (0,) * len(shape))


def _mix_prompt(proj, cos, sin, lbl, hn, w2, gb, gn, layer):
    bsz, seq, _ = proj.shape
    tile = lambda w: pl.BlockSpec((None, ROWS, w), lambda b, t: (b, t, 0))
    tab = pl.BlockSpec((ROWS, B_W), lambda b, t: (t, 0))
    st = lambda h, dk, dv: pl.BlockSpec((None, h, dk, dv), lambda b, t: (b, 0, 0, 0))
    return pl.pallas_call(
        functools.partial(_mix_prompt_kernel, layer=layer),
        grid=(bsz, seq // ROWS),
        in_specs=[tile(NP), tab, tab, _const((DEPTH, A_W)), _const((1, A_W)), _const((C_RP, C_QP)),
                  _const((1, C_QP)), _const((1, C_VW))],
        out_specs=[tile(D_MODEL), st(A_HEADS, A_DK, A_DV), st(B_HEADS, B_DK, B_DV), st(C_HEADS, C_DK, C_DV)],
        out_shape=[jax.ShapeDtypeStruct((bsz, seq, D_MODEL), BF16),
                   jax.ShapeDtypeStruct((bsz, A_HEADS, A_DK, A_DV), F32),
                   jax.ShapeDtypeStruct((bsz, B_HEADS, B_DK, B_DV), F32),
                   jax.ShapeDtypeStruct((bsz, C_HEADS, C_DK, C_DV), F32)],
        scratch_shapes=[pltpu.VMEM((A_W, A_W), F32), pltpu.VMEM((B_W, B_W), F32), pltpu.VMEM((C_QP, C_VW), F32)],
        compiler_params=pltpu.CompilerParams(dimension_semantics=("parallel", "arbitrary"),
                                             vmem_limit_bytes=VMEM_LIMIT),
        name="mix_prompt",
    )(proj, cos, sin, lbl, hn, w2, gb, gn)


def _mix_sample(proj, cos, sin, lbl, hn, w2, gb, gn, sa, sb, sc, layer, seq_len):
    m = proj.shape[0]
    n_seq = ROWS // seq_len
    tile = lambda w: pl.BlockSpec((ROWS, w), lambda i: (i, 0))
    st = lambda h, dk, dv: pl.BlockSpec((n_seq, h, dk, dv), lambda i: (i, 0, 0, 0))
    states = [st(A_HEADS, A_DK, A_DV), st(B_HEADS, B_DK, B_DV), st(C_HEADS, C_DK, C_DV)]
    return pl.pallas_call(
        functools.partial(_mix_sample_kernel, layer=layer, seq_len=seq_len),
        grid=(m // ROWS,),
        in_specs=[tile(NP), _const((ROWS, B_W)), _const((ROWS, B_W)), _const((DEPTH, A_W)), _const((1, A_W)),
                  _const((C_RP, C_QP)), _const((1, C_QP)), _const((1, C_VW))] + states,
        out_specs=[tile(D_MODEL)] + states,
        out_shape=[jax.ShapeDtypeStruct((m, D_MODEL), BF16), jax.ShapeDtypeStruct(sa.shape, F32),
                   jax.ShapeDtypeStruct(sb.shape, F32), jax.ShapeDtypeStruct(sc.shape, F32)],
        compiler_params=pltpu.CompilerParams(dimension_semantics=("parallel",), vmem_limit_bytes=VMEM_LIMIT),
        name="mix_sample",
    )(proj, cos, sin, lbl, hn, w2, gb, gn, sa, sb, sc)


def _rope_tables(pos):
    half = B_DK // 2
    inv_freq = ROPE_BASE ** (-jnp.arange(half, dtype=F32) / half)
    ang = pos[:, None] * inv_freq[None, :]
    cos, sin = jnp.cos(ang), jnp.sin(ang)
    cos_h = jnp.concatenate([cos, cos], axis=1)
    sin_h = jnp.concatenate([-sin, sin], axis=1)
    return jnp.tile(cos_h, (1, B_HEADS)), jnp.tile(sin_h, (1, B_HEADS))


def _pad_w_in(w):
    z = lambda n: jnp.zeros((w.shape[0], n), w.dtype)
    c0 = 2560
    return jnp.concatenate([
        w[:, :c0],
        w[:, c0:c0 + C_QW], z(C_QP - C_QW),
        w[:, c0 + C_QW:c0 + 2 * C_QW], z(C_QP - C_QW),
        w[:, c0 + 2 * C_QW:c0 + 2 * C_QW + 2 * C_VW],
        w[:, c0 + 2 * C_QW + 2 * C_VW:], z(C_RP - C_RANK)], axis=1)


def kernel(x_prompt, x_sample, state_hgrn, state_ret, state_gla, ffn1_norm, ffn1_w_in, ffn1_w_out, mix_norm, w_in, hgrn_lb_logits, hgrn_norm, gla_w2, gla_b, gla_norm, w_out, ffn2_norm, ffn2_w_in, ffn2_w_out, final_norm):
    bsz, seq, _ = x_prompt.shape
    dbs, dseq, _ = x_sample.shape
    assert seq % ROWS == 0 and ROWS % dseq == 0 and (bsz * seq) % TM == 0 and (dbs * dseq) % TM == 0

    row = lambda a: a.reshape(1, -1).astype(F32)
    f1_in, f1_out = ffn1_w_in.astype(BF16), ffn1_w_out.astype(BF16)
    f2_in, f2_out = ffn2_w_in.astype(BF16), ffn2_w_out.astype(BF16)
    wo = w_out.astype(BF16)
    wmix = [_pad_w_in(w_in[l]).astype(BF16) for l in range(DEPTH)]
    w2 = [jnp.zeros((C_RP, C_QP), BF16).at[:C_RANK, :C_QW].set(gla_w2[l].astype(BF16)) for l in range(DEPTH)]
    gb = [jnp.zeros((1, C_QP), F32).at[0, :C_QW].set(gla_b[l]) for l in range(DEPTH)]
    hn = [row(jnp.tile(hgrn_norm[l], A_HEADS)) for l in range(DEPTH)]
    gn = [row(jnp.tile(gla_norm[l], C_HEADS)) for l in range(DEPTH)]
    lbl = hgrn_lb_logits.astype(F32)

    cos_p, sin_p = _rope_tables(jnp.arange(seq, dtype=F32))
    cos_s, sin_s = _rope_tables(PAST_LEN + jnp.arange(dseq, dtype=F32))
    cos_s, sin_s = jnp.tile(cos_s, (ROWS // dseq, 1)), jnp.tile(sin_s, (ROWS // dseq, 1))

    def layer_tail(x1, y, l):
        return _out_ffn(x1, y, wo[l], row(ffn2_norm[l]), f2_in[l], f2_out[l], row(final_norm), l == DEPTH - 1)

    x = x_prompt.reshape(bsz * seq, D_MODEL)
    p_states = []
    for l in range(DEPTH):
        x1, proj = _ffn_proj(x, row(ffn1_norm[l]), f1_in[l], f1_out[l], row(mix_norm[l]), wmix[l])
        y, sa, sb, sc = _mix_prompt(proj.reshape(bsz, seq, NP), cos_p, sin_p, lbl, hn[l], w2[l], gb[l], gn[l], l)
        x = layer_tail(x1, y.reshape(bsz * seq, D_MODEL), l)
        p_states.append((sa, sb, sc))
    y_prompt = x.reshape(bsz, seq, D_MODEL)

    x = x_sample.reshape(dbs * dseq, D_MODEL)
    s_states = []
    for l in range(DEPTH):
        x1, proj = _ffn_proj(x, row(ffn1_norm[l]), f1_in[l], f1_out[l], row(mix_norm[l]), wmix[l])
        y, sa, sb, sc = _mix_sample(proj, cos_s, sin_s, lbl, hn[l], w2[l], gb[l], gn[l],
                                    state_hgrn[l], state_ret[l], state_gla[l], l, dseq)
        x = layer_tail(x1, y, l)
        s_states.append((sa, sb, sc))
    y_sample = x.reshape(dbs, dseq, D_MODEL)

    stack = lambda states, i: jnp.stack([s[i] for s in states])
    return (y_prompt, y_sample, stack(p_states, 0), stack(p_states, 1), stack(p_states, 2),
            stack(s_states, 0), stack(s_states, 1), stack(s_states, 2))
```

```python
import functools
import math

import jax
import jax.numpy as jnp
from jax import lax
from jax.experimental import pallas as pl
from jax.experimental.pallas import tpu as pltpu

F32, BF16 = jnp.float32, jnp.bfloat16

D_MODEL = 1024
DEPTH = 2
A_HEADS, A_DK, A_DV = 4, 64, 64
B_HEADS, B_DK, B_DV = 6, 64, 64
C_HEADS, C_DK, C_DV = 4, 48, 96
C_RANK = 16
C_TAU = 16.0
D_FF = 2816
ROPE_BASE = 10000.0
EPS = 1e-6
PAST_LEN = 16384

A_W = A_HEADS * A_DK
B_W = B_HEADS * B_DK
C_QW = C_HEADS * C_DK
C_VW = C_HEADS * C_DV
LANES = 128
MXU_DIM = 256
C_QP = 256
C_RP = LANES

O_AQ, O_AF, O_AI, O_AG = 0, 256, 512, 768
O_BQ, O_BK, O_BV, O_BG = 1024, 1408, 1792, 2176
O_CQ, O_CK, O_CV, O_CG, O_CR = 2560, 2816, 3072, 3456, 3840
NP = O_CR + C_RP

ROWS = 128
FACTOR_BLOCK = 64
FACTOR_MAX_EXP = 80.0
FF_CHUNK = 256
TM = 256
VMEM_LIMIT = 56 * 1024 * 1024


def _dot(a, b):
    return jnp.dot(a, b, preferred_element_type=F32)


def _dot_nt(a, b):
    return lax.dot_general(a, b, (((1,), (1,)), ((), ())), preferred_element_type=F32)


def _dot_tn(a, b):
    return lax.dot_general(a, b, (((0,), (0,)), ((), ())), preferred_element_type=F32)


def _iota(shape, dim):
    return lax.broadcasted_iota(jnp.int32, shape, dim)


def _split3(x):
    hi = x.astype(BF16)
    r = x - hi.astype(F32)
    mid = r.astype(BF16)
    lo = (r - mid.astype(F32)).astype(BF16)
    return hi, mid, lo


def _dot01(mat01, x):
    hi, mid, lo = _split3(x)
    return _dot(mat01, hi) + _dot(mat01, mid) + _dot(mat01, lo)


def _rms(x, w):
    ms = jnp.mean(x * x, axis=-1, keepdims=True)
    return x * lax.rsqrt(ms + EPS) * w


def _silu(x):
    return x * jax.nn.sigmoid(x)


def _swiglu(h_bf, win_ref, wout_ref, act_ref):
    for c in range(D_FF // FF_CHUNK):
        lo = c * FF_CHUNK
        g = _dot(h_bf, win_ref[:, lo:lo + FF_CHUNK])
        u = _dot(h_bf, win_ref[:, D_FF + lo:D_FF + lo + FF_CHUNK])
        act_ref[:, lo:lo + FF_CHUNK] = (_silu(g) * u).astype(BF16)
    return _dot(act_ref[...], wout_ref[...])


def _ffn_proj_kernel(x_ref, n1_ref, win_ref, wout_ref, nm_ref, wmix_ref, x1_ref, proj_ref, act_ref):
    x = x_ref[...]
    h = _rms(x, n1_ref[...]).astype(BF16)
    x1 = x + 0.5 * _swiglu(h, win_ref, wout_ref, act_ref)
    x1_ref[...] = x1
    hm = _rms(x1, nm_ref[...]).astype(BF16)
    proj_ref[...] = _dot(hm, wmix_ref[...])


def _out_ffn_kernel(x_ref, y_ref, wo_ref, n2_ref, win_ref, wout_ref, nf_ref, o_ref, act_ref, *, final):
    x2 = x_ref[...] + _dot(y_ref[...], wo_ref[...])
    h = _rms(x2, n2_ref[...]).astype(BF16)
    x3 = x2 + 0.5 * _swiglu(h, win_ref, wout_ref, act_ref)
    if final:
        x3 = _rms(x3, nf_ref[...])
    o_ref[...] = x3


def _resident(shape, layer=None):
    if layer is None:
        return pl.BlockSpec(shape, lambda i: (0,) * len(shape), pipeline_mode=pl.Buffered(1))
    return pl.BlockSpec((None,) + shape, lambda i: (layer,) + (0,) * len(shape), pipeline_mode=pl.Buffered(1))


def _ffn_proj(x, n1, win, wout, nm, wmix, layer):
    m = x.shape[0]
    row = lambda w: pl.BlockSpec((TM, w), lambda i: (i, 0))
    return pl.pallas_call(
        _ffn_proj_kernel,
        grid=(m // TM,),
        in_specs=[row(D_MODEL), _resident((1, D_MODEL), layer), _resident((D_MODEL, 2 * D_FF), layer),
                  _resident((D_FF, D_MODEL), layer), _resident((1, D_MODEL), layer),
                  _resident((D_MODEL, NP), layer)],
        out_specs=[row(D_MODEL), row(NP)],
        out_shape=[jax.ShapeDtypeStruct((m, D_MODEL), F32), jax.ShapeDtypeStruct((m, NP), F32)],
        scratch_shapes=[pltpu.VMEM((TM, D_FF), BF16)],
        compiler_params=pltpu.CompilerParams(dimension_semantics=("parallel",), vmem_limit_bytes=VMEM_LIMIT),
        name="ffn1_proj",
    )(x, n1, win, wout, nm, wmix)


def _out_ffn(x, y, wo, n2, win, wout, nf, layer):
    m = x.shape[0]
    row = lambda w: pl.BlockSpec((TM, w), lambda i: (i, 0))
    return pl.pallas_call(
        functools.partial(_out_ffn_kernel, final=layer == DEPTH - 1),
        grid=(m // TM,),
        in_specs=[row(D_MODEL), row(D_MODEL), _resident((D_MODEL, D_MODEL), layer),
                  _resident((1, D_MODEL), layer), _resident((D_MODEL, 2 * D_FF), layer),
                  _resident((D_FF, D_MODEL), layer), _resident((1, D_MODEL))],
        out_specs=row(D_MODEL),
        out_shape=jax.ShapeDtypeStruct((m, D_MODEL), F32),
        scratch_shapes=[pltpu.VMEM((TM, D_FF), BF16)],
        compiler_params=pltpu.CompilerParams(dimension_semantics=("parallel",), vmem_limit_bytes=VMEM_LIMIT),
        name="outproj_ffn2",
    )(x, y, wo, n2, win, wout, nf)


def _log2(n):
    assert n & (n - 1) == 0
    return n.bit_length() - 1


def _head_mask(width, dh, h):
    lane = _iota((1, width), 1)
    return (lane >= dh * h) & (lane < dh * (h + 1))


def _head_grid_mask(rows, cols, dr, dc, n_heads):
    r = _iota((rows, cols), 0)
    c = _iota((rows, cols), 1)
    m = None
    for h in range(n_heads):
        mh = (r >= dr * h) & (r < dr * (h + 1)) & (c >= dc * h) & (c < dc * (h + 1))
        m = mh if m is None else (m | mh)
    return m


def _seg_mean_sq(o, dh, n_heads):
    w = o.shape[1]
    if w > MXU_DIM and MXU_DIM % dh == 0:
        parts = [_seg_mean_sq(o[:, lo:min(lo + MXU_DIM, w)], dh, (min(lo + MXU_DIM, w) - lo) // dh)
                 for lo in range(0, w, MXU_DIM)]
        return jnp.concatenate(parts, axis=1)
    ones = jnp.where(_head_grid_mask(w, w, dh, dh, n_heads), 1.0, 0.0).astype(BF16)
    sq = o * o
    hi = sq.astype(BF16)
    lo = (sq - hi.astype(F32)).astype(BF16)
    return (_dot(hi, ones) + _dot(lo, ones)) * (1.0 / dh)


def _head_norm(o, dh, n_heads):
    return o * lax.rsqrt(_seg_mean_sq(o, dh, n_heads) + EPS)


def _seq_cumsum(g, seq_len):
    rows = _iota((ROWS, ROWS), 0)
    cols = _iota((ROWS, ROWS), 1)
    sh = _log2(seq_len)
    tril = jnp.where(((rows >> sh) == (cols >> sh)) & (cols <= rows), 1.0, 0.0).astype(BF16)
    return _dot01(tril, g)


def _block_row(b, size, idx):
    c = b.shape[1]
    if size >= 8:
        b3 = b.reshape(ROWS // size, size, c)
        return jnp.broadcast_to(b3[:, idx:idx + 1, :], b3.shape).reshape(ROWS, c)
    b8 = b.reshape(ROWS // 8, 8, c)
    sub = _iota(b8.shape, 1)
    ref = b8[:, idx:idx + 1, :]
    for blk in range(1, 8 // size):
        lo = blk * size
        ref = jnp.where(sub >= lo, b8[:, lo + idx:lo + idx + 1, :], ref)
    return jnp.broadcast_to(ref, b8.shape).reshape(ROWS, c)


def _stack_heads(x, dh, n_heads, extra_mask=None):
    parts = []
    for h in range(n_heads):
        m = _head_mask(x.shape[1], dh, h)
        if extra_mask is not None:
            m = m & extra_mask
        parts.append(jnp.where(m, x, 0.0))
    return jnp.concatenate(parts, axis=0).astype(BF16)


def _apply_scores(p, v, dv, n_heads):
    pcat = jnp.concatenate([p[h].astype(BF16) for h in range(n_heads)], axis=1)
    return _dot(pcat, _stack_heads(v, dv, n_heads))


def _tree_levels(q, k, b, dk, n_heads, first, seq_len):
    rows = _iota((ROWS, ROWS), 0)
    cols = _iota((ROWS, ROWS), 1)
    rowc = _iota((ROWS, q.shape[1]), 0)
    total = jnp.zeros((n_heads, ROWS, ROWS), F32)
    m = first
    while m < seq_len:
        ref = _block_row(b, 2 * m, m - 1)
        up = (rowc & (2 * m - 1)) >= m
        e = jnp.exp(jnp.where(up, b - ref, ref - b))
        ql = _stack_heads(q * e, dk, n_heads, extra_mask=up)
        kl = jnp.where(up, 0.0, k * e).astype(BF16)
        sc = _dot_nt(ql, kl).reshape(n_heads, ROWS, ROWS)
        sh = _log2(2 * m)
        total = total + jnp.where(((rows >> sh) == (cols >> sh))[None], sc, 0.0)
        m *= 2
    return total


def _gated_intra(q, k, v, b, dk, dv, n_heads, seq_len):
    rows = _iota((ROWS, ROWS), 0)
    cols = _iota((ROWS, ROWS), 1)
    blk = min(FACTOR_BLOCK, seq_len)
    dev = b - _block_row(b, blk, blk // 2 - 1)

    def factorised():
        qf = _stack_heads(q * jnp.exp(dev), dk, n_heads)
        kf = (k * jnp.exp(-dev)).astype(BF16)
        sc = _dot_nt(qf, kf).reshape(n_heads, ROWS, ROWS)
        sh = _log2(blk)
        keep = ((rows >> sh) == (cols >> sh)) & (cols <= rows)
        return jnp.where(keep[None], sc, 0.0) + _tree_levels(q, k, b, dk, n_heads, blk, seq_len)

    def tree():
        sc = _dot_nt(_stack_heads(q, dk, n_heads), k.astype(BF16)).reshape(n_heads, ROWS, ROWS)
        return jnp.where((rows == cols)[None], sc, 0.0) + _tree_levels(q, k, b, dk, n_heads, 1, seq_len)

    total = lax.cond(jnp.max(jnp.abs(dev)) < FACTOR_MAX_EXP, factorised, tree)
    return _apply_scores(total, v, dv, n_heads)


def _column(row_vec):
    c = row_vec.shape[1]
    return jnp.transpose(jnp.broadcast_to(row_vec, (8, c)))[:, 0:1]


def _gated_prompt(q, k, v, g, dk, dv, n_heads, s_ref):
    b = _seq_cumsum(g, ROWS)
    o = _gated_intra(q, k, v, b, dk, dv, n_heads, ROWS)
    s0 = s_ref[...]
    o = o + _dot((q * jnp.exp(b)).astype(BF16), s0.astype(BF16))
    b_last = b[ROWS - 1:ROWS, :]
    khat = (k * jnp.exp(b_last - b)).astype(BF16)
    ds = _dot_tn(khat, v.astype(BF16))
    keep = _head_grid_mask(q.shape[1], v.shape[1], dk, dv, n_heads)
    s_ref[...] = s0 * _column(jnp.exp(b_last)) + jnp.where(keep, ds, 0.0)
    return o


def _gated_sample(q, k, v, g, dk, dv, n_heads, seq_len, s_ref, snew_ref):
    n_seq = ROWS // seq_len
    b = _seq_cumsum(g, seq_len)
    o = _gated_intra(q, k, v, b, dk, dv, n_heads, seq_len)
    qhat = (q * jnp.exp(b)).astype(BF16)
    b_last = _block_row(b, seq_len, seq_len - 1)
    khat = (k * jnp.exp(b_last - b)).astype(BF16)
    decay_t = jnp.exp(jnp.transpose(b))
    v_bf = v.astype(BF16)
    o_inter = []
    for h in range(n_heads):
        s0 = s_ref[:, h]
        qh = qhat[:, dk * h:dk * (h + 1)].reshape(n_seq, seq_len, dk)
        oi = jnp.einsum('bqd,bdv->bqv', qh, s0.astype(BF16), preferred_element_type=F32)
        o_inter.append(oi.reshape(ROWS, dv))
        kh = khat[:, dk * h:dk * (h + 1)].reshape(n_seq, seq_len, dk)
        vh = v_bf[:, dv * h:dv * (h + 1)].reshape(n_seq, seq_len, dv)
        ds = jnp.einsum('bsd,bsv->bdv', kh, vh, preferred_element_type=F32)
        for s in range(n_seq):
            last = seq_len * s + seq_len - 1
            col = decay_t[dk * h:dk * (h + 1), last:last + 1]
            snew_ref[s, h] = col * s0[s] + ds[s]
    pad = v.shape[1] - n_heads * dv
    if pad:
        o_inter.append(jnp.zeros((ROWS, pad), F32))
    return o + jnp.concatenate(o_inter, axis=1)


_LOG_GAMMA = tuple(math.log(1.0 - 2.0 ** (-5.0 - h)) for h in range(B_HEADS))


def _per_head_lanes(width, dh, values):
    lane = _iota((1, width), 1)
    out = jnp.zeros((1, width), F32)
    for h, val in enumerate(values):
        out = jnp.where((lane >= dh * h) & (lane < dh * (h + 1)), val, out)
    return out


def _rotary(x, cos, sin_signed):
    w = x.shape[1]
    lane = _iota((1, w), 1)
    first_half = (lane & (B_DK - 1)) < (B_DK // 2)
    partner = jnp.where(first_half, pltpu.roll(x, w - B_DK // 2, 1), pltpu.roll(x, B_DK // 2, 1))
    return x * cos + partner * sin_signed


def _lane_groups(width):
    return [(lo, min(lo + MXU_DIM, width)) for lo in range(0, width, MXU_DIM)]


def _retention_intra(q, k, v, seq_len):
    rows = _iota((ROWS, ROWS), 0)
    cols = _iota((ROWS, ROWS), 1)
    sh = _log2(seq_len)
    causal = ((rows >> sh) == (cols >> sh)) & (cols <= rows)
    dist = (rows - cols).astype(F32)
    outs = []
    for lo, hi in _lane_groups(B_W):
        nh = (hi - lo) // B_DK
        sc = _dot_nt(_stack_heads(q[:, lo:hi], B_DK, nh), k[:, lo:hi].astype(BF16)).reshape(nh, ROWS, ROWS)
        gam = jnp.stack([jnp.where(causal, jnp.exp(dist * lg), 0.0)
                         for lg in _LOG_GAMMA[lo // B_DK:hi // B_DK]])
        outs.append(_apply_scores(sc * gam, v[:, lo:hi], B_DV, nh))
    return jnp.concatenate(outs, axis=1)


def _retention_scaled(q, k, seq_len):
    tau = (_iota((ROWS, 1), 0) & (seq_len - 1)).astype(F32)
    lg = _per_head_lanes(B_W, B_DK, _LOG_GAMMA)
    qhat = (q * jnp.exp(lg * (tau + 1.0))).astype(BF16)
    khat = (k * jnp.exp(lg * (seq_len - 1.0 - tau))).astype(BF16)
    return qhat, khat


def _retention_prompt(q, k, v, s_ref):
    o = _retention_intra(q, k, v, ROWS)
    qhat, khat = _retention_scaled(q, k, ROWS)
    decay = _column(_per_head_lanes(B_W, B_DK, [math.exp(ROWS * lg) for lg in _LOG_GAMMA]))
    v_bf = v.astype(BF16)
    o_inter = []
    for lo, hi in _lane_groups(B_W):
        s0 = s_ref[lo:hi, lo:hi]
        o_inter.append(_dot(qhat[:, lo:hi], s0.astype(BF16)))
        ds = _dot_tn(khat[:, lo:hi], v_bf[:, lo:hi])
        keep = _head_grid_mask(hi - lo, hi - lo, B_DK, B_DV, (hi - lo) // B_DK)
        s_ref[lo:hi, lo:hi] = s0 * decay[lo:hi, :] + jnp.where(keep, ds, 0.0)
    return o + jnp.concatenate(o_inter, axis=1)


def _retention_sample(q, k, v, seq_len, s_ref, snew_ref):
    n_seq = ROWS // seq_len
    o = _retention_intra(q, k, v, seq_len)
    qhat, khat = _retention_scaled(q, k, seq_len)
    v_bf = v.astype(BF16)
    o_inter = []
    for h in range(B_HEADS):
        s0 = s_ref[:, h]
        qh = qhat[:, B_DK * h:B_DK * (h + 1)].reshape(n_seq, seq_len, B_DK)
        oi = jnp.einsum('bqd,bdv->bqv', qh, s0.astype(BF16), preferred_element_type=F32)
        o_inter.append(oi.reshape(ROWS, B_DV))
        kh = khat[:, B_DK * h:B_DK * (h + 1)].reshape(n_seq, seq_len, B_DK)
        vh = v_bf[:, B_DV * h:B_DV * (h + 1)].reshape(n_seq, seq_len, B_DV)
        ds = jnp.einsum('bsd,bsv->bdv', kh, vh, preferred_element_type=F32)
        snew_ref[:, h] = s0 * math.exp(seq_len * _LOG_GAMMA[h]) + ds
    return o + jnp.concatenate(o_inter, axis=1)


def _lower_bound(lbl_ref, layer):
    rows = [lbl_ref[i:i + 1, :] for i in range(DEPTH)]
    mx = functools.reduce(jnp.maximum, rows)
    ex = [jnp.exp(r - mx) for r in rows]
    tot = functools.reduce(lambda a, c: a + c, ex)
    acc = functools.reduce(lambda a, c: a + c, ex[:layer + 1]) / tot
    return acc - ex[0] / tot


def _mix_tile(proj_ref, cos_ref, sin_ref, lbl_ref, hn_ref, w2_ref, gb_ref, gn_ref, layer, gated, retention):
    seg = lambda off, w: proj_ref[:, off:off + w]
    af = seg(O_AF, A_W)
    if layer == 0:
        log_f = jax.nn.log_sigmoid(af)
        k_a = jax.nn.sigmoid(-af)
    else:
        lb = _lower_bound(lbl_ref, layer)
        log_f = jnp.logaddexp(jnp.log(lb), jnp.log1p(-lb) + jax.nn.log_sigmoid(af))
        k_a = (1.0 - lb) * jax.nn.sigmoid(-af)
    o_a = gated("a", _silu(seg(O_AQ, A_W)), k_a, seg(O_AI, A_W), log_f, A_DK, A_DV, A_HEADS)
    y_a = _head_norm(o_a, A_DV, A_HEADS) * hn_ref[...] * _silu(seg(O_AG, A_W))
    q_b = _rotary(seg(O_BQ, B_W), cos_ref[...], sin_ref[...])
    k_b = _rotary(seg(O_BK, B_W), cos_ref[...], sin_ref[...]) * (B_DK ** -0.5)
    o_b = retention(q_b, k_b, seg(O_BV, B_W))
    y_b = _head_norm(o_b, B_DV, B_HEADS) * _silu(seg(O_BG, B_W))
    gk = _dot(seg(O_CR, C_RP).astype(BF16), w2_ref[...]) + gb_ref[...]
    log_a = jax.nn.log_sigmoid(gk) * (1.0 / C_TAU)
    o_c = gated("c", seg(O_CQ, C_QP) * (C_DK ** -0.5), seg(O_CK, C_QP), seg(O_CV, C_VW), log_a,
                C_DK, C_DV, C_HEADS)
    y_c = _head_norm(o_c, C_DV, C_HEADS) * gn_ref[...] * _silu(seg(O_CG, C_VW))
    return jnp.concatenate([y_a, y_b, y_c], axis=1).astype(BF16)


def _mix_prompt_kernel(proj_ref, cos_ref, sin_ref, lbl_ref, hn_ref, w2_ref, gb_ref, gn_ref,
                       y_ref, sa_ref, sb_ref, sc_ref, sa_acc, sb_acc, sc_acc, *, layer):
    t = pl.program_id(1)

    @pl.when(t == 0)
    def _():
        sa_acc[...] = jnp.zeros_like(sa_acc)
        sb_acc[...] = jnp.zeros_like(sb_acc)
        sc_acc[...] = jnp.zeros_like(sc_acc)

    def gated(which, q, k, v, g, dk, dv, n_heads):
        return _gated_prompt(q, k, v, g, dk, dv, n_heads, sa_acc if which == "a" else sc_acc)

    def retention(q, k, v):
        return _retention_prompt(q, k, v, sb_acc)

    y_ref[...] = _mix_tile(proj_ref, cos_ref, sin_ref, lbl_ref, hn_ref, w2_ref, gb_ref, gn_ref, layer,
                           gated, retention)

    @pl.when(t == pl.num_programs(1) - 1)
    def _():
        for h in range(A_HEADS):
            sa_ref[h] = sa_acc[A_DK * h:A_DK * (h + 1), A_DV * h:A_DV * (h + 1)]
        for h in range(B_HEADS):
            sb_ref[h] = sb_acc[B_DK * h:B_DK * (h + 1), B_DV * h:B_DV * (h + 1)]
        for h in range(C_HEADS):
            sc_ref[h] = sc_acc[C_DK * h:C_DK * (h + 1), C_DV * h:C_DV * (h + 1)]


def _mix_sample_kernel(proj_ref, cos_ref, sin_ref, lbl_ref, hn_ref, w2_ref, gb_ref, gn_ref,
                       sa_ref, sb_ref, sc_ref, y_ref, sa_new, sb_new, sc_new, *, layer, seq_len):
    def gated(which, q, k, v, g, dk, dv, n_heads):
        s_ref, snew_ref = (sa_ref, sa_new) if which == "a" else (sc_ref, sc_new)
        return _gated_sample(q, k, v, g, dk, dv, n_heads, seq_len, s_ref, snew_ref)

    def retention(q, k, v):
        return _retention_sample(q, k, v, seq_len, sb_ref, sb_new)

    y_ref[...] = _mix_tile(proj_ref, cos_ref, sin_ref, lbl_ref, hn_ref, w2_ref, gb_ref, gn_ref, layer,
                           gated, retention)


def _const(shape):
    return pl.BlockSpec(shape, lambda *_: (0,) * len(shape))


def _mix_prompt(proj, cos, sin, lbl, hn, w2, gb, gn, layer):
    bsz, seq, _ = proj.shape
    tile = lambda w: pl.BlockSpec((None, ROWS, w), lambda b, t: (b, t, 0))
    tab = pl.BlockSpec((ROWS, B_W), lambda b, t: (t, 0))
    st = lambda h, dk, dv: pl.BlockSpec((None, h, dk, dv), lambda b, t: (b, 0, 0, 0))
    return pl.pallas_call(
        functools.partial(_mix_prompt_kernel, layer=layer),
        grid=(bsz, seq // ROWS),
        in_specs=[tile(NP), tab, tab, _const((DEPTH, A_W)), _const((1, A_W)), _const((C_RP, C_QP)),
                  _const((1, C_QP)), _const((1, C_VW))],
        out_specs=[tile(D_MODEL), st(A_HEADS, A_DK, A_DV), st(B_HEADS, B_DK, B_DV), st(C_HEADS, C_DK, C_DV)],
        out_shape=[jax.ShapeDtypeStruct((bsz, seq, D_MODEL), BF16),
                   jax.ShapeDtypeStruct((bsz, A_HEADS, A_DK, A_DV), F32),
                   jax.ShapeDtypeStruct((bsz, B_HEADS, B_DK, B_DV), F32),
                   jax.ShapeDtypeStruct((bsz, C_HEADS, C_DK, C_DV), F32)],
        scratch_shapes=[pltpu.VMEM((A_W, A_W), F32), pltpu.VMEM((B_W, B_W), F32), pltpu.VMEM((C_QP, C_VW), F32)],
        compiler_params=pltpu.CompilerParams(dimension_semantics=("parallel", "arbitrary"),
                                             vmem_limit_bytes=VMEM_LIMIT),
        name="mix_prompt",
    )(proj, cos, sin, lbl, hn, w2, gb, gn)


def _mix_sample(proj, cos, sin, lbl, hn, w2, gb, gn, sa, sb, sc, layer, seq_len):
    m = proj.shape[0]
    n_seq = ROWS // seq_len
    tile = lambda w: pl.BlockSpec((ROWS, w), lambda i: (i, 0))
    st_in = lambda h, dk, dv: pl.BlockSpec((None, n_seq, h, dk, dv), lambda i: (layer, i, 0, 0, 0))
    st_out = lambda h, dk, dv: pl.BlockSpec((n_seq, h, dk, dv), lambda i: (i, 0, 0, 0))
    dims = [(A_HEADS, A_DK, A_DV), (B_HEADS, B_DK, B_DV), (C_HEADS, C_DK, C_DV)]
    return pl.pallas_call(
        functools.partial(_mix_sample_kernel, layer=layer, seq_len=seq_len),
        grid=(m // ROWS,),
        in_specs=[tile(NP), _const((ROWS, B_W)), _const((ROWS, B_W)), _const((DEPTH, A_W)), _const((1, A_W)),
                  _const((C_RP, C_QP)), _const((1, C_QP)), _const((1, C_VW))] + [st_in(*d) for d in dims],
        out_specs=[tile(D_MODEL)] + [st_out(*d) for d in dims],
        out_shape=[jax.ShapeDtypeStruct((m, D_MODEL), BF16), jax.ShapeDtypeStruct(sa.shape[1:], F32),
                   jax.ShapeDtypeStruct(sb.shape[1:], F32), jax.ShapeDtypeStruct(sc.shape[1:], F32)],
        compiler_params=pltpu.CompilerParams(dimension_semantics=("parallel",), vmem_limit_bytes=VMEM_LIMIT),
        name="mix_sample",
    )(proj, cos, sin, lbl, hn, w2, gb, gn, sa, sb, sc)


def _rope_tables(pos):
    half = B_DK // 2
    inv_freq = ROPE_BASE ** (-jnp.arange(half, dtype=F32) / half)
    ang = pos[:, None] * inv_freq[None, :]
    cos, sin = jnp.cos(ang), jnp.sin(ang)
    cos_h = jnp.concatenate([cos, cos], axis=1)
    sin_h = jnp.concatenate([-sin, sin], axis=1)
    return jnp.tile(cos_h, (1, B_HEADS)), jnp.tile(sin_h, (1, B_HEADS))


def _pad_w_in(w):
    z = lambda n: jnp.zeros(w.shape[:2] + (n,), w.dtype)
    c0 = 2560
    return jnp.concatenate([
        w[..., :c0],
        w[..., c0:c0 + C_QW], z(C_QP - C_QW),
        w[..., c0 + C_QW:c0 + 2 * C_QW], z(C_QP - C_QW),
        w[..., c0 + 2 * C_QW:c0 + 2 * C_QW + 2 * C_VW],
        w[..., c0 + 2 * C_QW + 2 * C_VW:], z(C_RP - C_RANK)], axis=-1)


def kernel(x_prompt, x_sample, state_hgrn, state_ret, state_gla, ffn1_norm, ffn1_w_in, ffn1_w_out, mix_norm, w_in, hgrn_lb_logits, hgrn_norm, gla_w2, gla_b, gla_norm, w_out, ffn2_norm, ffn2_w_in, ffn2_w_out, final_norm):
    bsz, seq, _ = x_prompt.shape
    dbs, dseq, _ = x_sample.shape
    assert seq % ROWS == 0 and ROWS % dseq == 0 and (bsz * seq) % TM == 0 and (dbs * dseq) % TM == 0

    row = lambda a: a.reshape(1, -1).astype(F32)
    f1_in, f1_out = ffn1_w_in.astype(BF16), ffn1_w_out.astype(BF16)
    f2_in, f2_out = ffn2_w_in.astype(BF16), ffn2_w_out.astype(BF16)
    wo = w_out.astype(BF16)
    wmix = _pad_w_in(w_in.astype(BF16))
    stacked = lambda a: a.reshape(DEPTH, 1, -1).astype(F32)
    n1, nm, n2 = stacked(ffn1_norm), stacked(mix_norm), stacked(ffn2_norm)
    w2 = [jnp.zeros((C_RP, C_QP), BF16).at[:C_RANK, :C_QW].set(gla_w2[l].astype(BF16)) for l in range(DEPTH)]
    gb = [jnp.zeros((1, C_QP), F32).at[0, :C_QW].set(gla_b[l]) for l in range(DEPTH)]
    hn = [row(jnp.tile(hgrn_norm[l], A_HEADS)) for l in range(DEPTH)]
    gn = [row(jnp.tile(gla_norm[l], C_HEADS)) for l in range(DEPTH)]
    lbl = hgrn_lb_logits.astype(F32)

    cos_p, sin_p = _rope_tables(jnp.arange(seq, dtype=F32))
    cos_s, sin_s = _rope_tables(PAST_LEN + jnp.arange(dseq, dtype=F32))
    cos_s, sin_s = jnp.tile(cos_s, (ROWS // dseq, 1)), jnp.tile(sin_s, (ROWS // dseq, 1))

    def layer_tail(x1, y, l):
        return _out_ffn(x1, y, wo, n2, f2_in, f2_out, row(final_norm), l)

    x = x_prompt.reshape(bsz * seq, D_MODEL)
    p_states = []
    for l in range(DEPTH):
        x1, proj = _ffn_proj(x, n1, f1_in, f1_out, nm, wmix, l)
        y, sa, sb, sc = _mix_prompt(proj.reshape(bsz, seq, NP), cos_p, sin_p, lbl, hn[l], w2[l], gb[l], gn[l], l)
        x = layer_tail(x1, y.reshape(bsz * seq, D_MODEL), l)
        p_states.append((sa, sb, sc))
    y_prompt = x.reshape(bsz, seq, D_MODEL)

    x = x_sample.reshape(dbs * dseq, D_MODEL)
    s_states = []
    for l in range(DEPTH):
        x1, proj = _ffn_proj(x, n1, f1_in, f1_out, nm, wmix, l)
        y, sa, sb, sc = _mix_sample(proj, cos_s, sin_s, lbl, hn[l], w2[l], gb[l], gn[l],
                                    state_hgrn, state_ret, state_gla, l, dseq)
        x = layer_tail(x1, y, l)
        s_states.append((sa, sb, sc))
    y_sample = x.reshape(dbs, dseq, D_MODEL)

    stack = lambda states, i: jnp.stack([s[i] for s in states])
    return (y_prompt, y_sample, stack(p_states, 0), stack(p_states, 1), stack(p_states, 2),
            stack(s_states, 0), stack(s_states, 1), stack(s_states, 2))
```

```python
import functools
import math

import jax
import jax.numpy as jnp
from jax import lax
from jax.experimental import pallas as pl
from jax.experimental.pallas import tpu as pltpu

F32, BF16 = jnp.float32, jnp.bfloat16

D_MODEL = 1024
DEPTH = 2
A_HEADS, A_DK, A_DV = 4, 64, 64
B_HEADS, B_DK, B_DV = 6, 64, 64
C_HEADS, C_DK, C_DV = 4, 48, 96
C_RANK = 16
C_TAU = 16.0
D_FF = 2816
ROPE_BASE = 10000.0
EPS = 1e-6
PAST_LEN = 16384

A_W = A_HEADS * A_DK
B_W = B_HEADS * B_DK
C_QW = C_HEADS * C_DK
C_VW = C_HEADS * C_DV
LANES = 128
MXU_DIM = 256
C_QP = 256
C_RP = LANES

O_AQ, O_AF, O_AI, O_AG = 0, 256, 512, 768
O_BQ, O_BK, O_BV, O_BG = 1024, 1408, 1792, 2176
O_CQ, O_CK, O_CV, O_CG, O_CR = 2560, 2816, 3072, 3456, 3840
NP = O_CR + C_RP

ROWS = 128
SEQS_PER_STEP = 2
FACTOR_BLOCK = 64
FACTOR_MAX_EXP = 80.0
FF_CHUNK = 256
TM = 256
VMEM_LIMIT = 56 * 1024 * 1024


def _dot(a, b):
    return jnp.dot(a, b, preferred_element_type=F32)


def _dot_nt(a, b):
    return lax.dot_general(a, b, (((1,), (1,)), ((), ())), preferred_element_type=F32)


def _dot_tn(a, b):
    return lax.dot_general(a, b, (((0,), (0,)), ((), ())), preferred_element_type=F32)


def _iota(shape, dim):
    return lax.broadcasted_iota(jnp.int32, shape, dim)


def _split3(x):
    hi = x.astype(BF16)
    r = x - hi.astype(F32)
    mid = r.astype(BF16)
    lo = (r - mid.astype(F32)).astype(BF16)
    return hi, mid, lo


def _dot01(mat01, x):
    hi, mid, lo = _split3(x)
    return _dot(mat01, hi) + _dot(mat01, mid) + _dot(mat01, lo)


def _rms(x, w):
    ms = jnp.mean(x * x, axis=-1, keepdims=True)
    return x * lax.rsqrt(ms + EPS) * w


def _silu(x):
    return x * jax.nn.sigmoid(x)


def _swiglu(h_bf, win_ref, wout_ref, act_ref):
    for c in range(D_FF // FF_CHUNK):
        lo = c * FF_CHUNK
        g = _dot(h_bf, win_ref[:, lo:lo + FF_CHUNK])
        u = _dot(h_bf, win_ref[:, D_FF + lo:D_FF + lo + FF_CHUNK])
        act_ref[:, lo:lo + FF_CHUNK] = (_silu(g) * u).astype(BF16)
    return _dot(act_ref[...], wout_ref[...])


def _ffn_proj_kernel(x_ref, n1_ref, win_ref, wout_ref, nm_ref, wmix_ref, x1_ref, proj_ref, act_ref):
    x = x_ref[...]
    h = _rms(x, n1_ref[...]).astype(BF16)
    x1 = x + 0.5 * _swiglu(h, win_ref, wout_ref, act_ref)
    x1_ref[...] = x1
    hm = _rms(x1, nm_ref[...]).astype(BF16)
    proj_ref[...] = _dot(hm, wmix_ref[...])


def _out_ffn_kernel(x_ref, y_ref, wo_ref, n2_ref, win_ref, wout_ref, nf_ref, o_ref, act_ref, *, final):
    x2 = x_ref[...] + _dot(y_ref[...], wo_ref[...])
    h = _rms(x2, n2_ref[...]).astype(BF16)
    x3 = x2 + 0.5 * _swiglu(h, win_ref, wout_ref, act_ref)
    if final:
        x3 = _rms(x3, nf_ref[...])
    o_ref[...] = x3


def _resident(shape, layer=None):
    if layer is None:
        return pl.BlockSpec(shape, lambda i: (0,) * len(shape), pipeline_mode=pl.Buffered(1))
    return pl.BlockSpec((None,) + shape, lambda i: (layer,) + (0,) * len(shape), pipeline_mode=pl.Buffered(1))


def _ffn_proj(x, n1, win, wout, nm, wmix, layer):
    m = x.shape[0]
    row = lambda w: pl.BlockSpec((TM, w), lambda i: (i, 0))
    return pl.pallas_call(
        _ffn_proj_kernel,
        grid=(m // TM,),
        in_specs=[row(D_MODEL), _resident((1, D_MODEL), layer), _resident((D_MODEL, 2 * D_FF), layer),
                  _resident((D_FF, D_MODEL), layer), _resident((1, D_MODEL), layer),
                  _resident((D_MODEL, NP), layer)],
        out_specs=[row(D_MODEL), row(NP)],
        out_shape=[jax.ShapeDtypeStruct((m, D_MODEL), F32), jax.ShapeDtypeStruct((m, NP), F32)],
        scratch_shapes=[pltpu.VMEM((TM, D_FF), BF16)],
        compiler_params=pltpu.CompilerParams(dimension_semantics=("parallel",), vmem_limit_bytes=VMEM_LIMIT),
        name="ffn1_proj",
    )(x, n1, win, wout, nm, wmix)


def _out_ffn(x, y, wo, n2, win, wout, nf, layer):
    m = x.shape[0]
    row = lambda w: pl.BlockSpec((TM, w), lambda i: (i, 0))
    return pl.pallas_call(
        functools.partial(_out_ffn_kernel, final=layer == DEPTH - 1),
        grid=(m // TM,),
        in_specs=[row(D_MODEL), row(D_MODEL), _resident((D_MODEL, D_MODEL), layer),
                  _resident((1, D_MODEL), layer), _resident((D_MODEL, 2 * D_FF), layer),
                  _resident((D_FF, D_MODEL), layer), _resident((1, D_MODEL))],
        out_specs=row(D_MODEL),
        out_shape=jax.ShapeDtypeStruct((m, D_MODEL), F32),
        scratch_shapes=[pltpu.VMEM((TM, D_FF), BF16)],
        compiler_params=pltpu.CompilerParams(dimension_semantics=("parallel",), vmem_limit_bytes=VMEM_LIMIT),
        name="outproj_ffn2",
    )(x, y, wo, n2, win, wout, nf)


def _log2(n):
    assert n & (n - 1) == 0
    return n.bit_length() - 1


def _head_mask(width, dh, h):
    lane = _iota((1, width), 1)
    return (lane >= dh * h) & (lane < dh * (h + 1))


def _head_grid_mask(rows, cols, dr, dc, n_heads):
    r = _iota((rows, cols), 0)
    c = _iota((rows, cols), 1)
    m = None
    for h in range(n_heads):
        mh = (r >= dr * h) & (r < dr * (h + 1)) & (c >= dc * h) & (c < dc * (h + 1))
        m = mh if m is None else (m | mh)
    return m


def _seg_mean_sq(o, dh, n_heads):
    w = o.shape[1]
    if w > MXU_DIM and MXU_DIM % dh == 0:
        parts = [_seg_mean_sq(o[:, lo:min(lo + MXU_DIM, w)], dh, (min(lo + MXU_DIM, w) - lo) // dh)
                 for lo in range(0, w, MXU_DIM)]
        return jnp.concatenate(parts, axis=1)
    ones = jnp.where(_head_grid_mask(w, w, dh, dh, n_heads), 1.0, 0.0).astype(BF16)
    sq = o * o
    hi = sq.astype(BF16)
    lo = (sq - hi.astype(F32)).astype(BF16)
    return (_dot(hi, ones) + _dot(lo, ones)) * (1.0 / dh)


def _head_norm(o, dh, n_heads):
    return o * lax.rsqrt(_seg_mean_sq(o, dh, n_heads) + EPS)


def _seq_cumsum(g, seq_len):
    rows = _iota((ROWS, ROWS), 0)
    cols = _iota((ROWS, ROWS), 1)
    sh = _log2(seq_len)
    tril = jnp.where(((rows >> sh) == (cols >> sh)) & (cols <= rows), 1.0, 0.0).astype(BF16)
    return _dot01(tril, g)


def _block_row(b, size, idx):
    c = b.shape[1]
    if size >= 8:
        b3 = b.reshape(ROWS // size, size, c)
        return jnp.broadcast_to(b3[:, idx:idx + 1, :], b3.shape).reshape(ROWS, c)
    b8 = b.reshape(ROWS // 8, 8, c)
    sub = _iota(b8.shape, 1)
    ref = b8[:, idx:idx + 1, :]
    for blk in range(1, 8 // size):
        lo = blk * size
        ref = jnp.where(sub >= lo, b8[:, lo + idx:lo + idx + 1, :], ref)
    return jnp.broadcast_to(ref, b8.shape).reshape(ROWS, c)


def _stack_heads(x, dh, n_heads, extra_mask=None):
    parts = []
    for h in range(n_heads):
        m = _head_mask(x.shape[1], dh, h)
        if extra_mask is not None:
            m = m & extra_mask
        parts.append(jnp.where(m, x, 0.0))
    return jnp.concatenate(parts, axis=0).astype(BF16)


def _apply_scores(p, v, dv, n_heads):
    pcat = jnp.concatenate([p[h].astype(BF16) for h in range(n_heads)], axis=1)
    return _dot(pcat, _stack_heads(v, dv, n_heads))


def _tree_levels(q, k, b, dk, n_heads, first, seq_len):
    rows = _iota((ROWS, ROWS), 0)
    cols = _iota((ROWS, ROWS), 1)
    rowc = _iota((ROWS, q.shape[1]), 0)
    total = jnp.zeros((n_heads, ROWS, ROWS), F32)
    m = first
    while m < seq_len:
        ref = _block_row(b, 2 * m, m - 1)
        up = (rowc & (2 * m - 1)) >= m
        e = jnp.exp(jnp.where(up, b - ref, ref - b))
        ql = _stack_heads(q * e, dk, n_heads, extra_mask=up)
        kl = jnp.where(up, 0.0, k * e).astype(BF16)
        sc = _dot_nt(ql, kl).reshape(n_heads, ROWS, ROWS)
        sh = _log2(2 * m)
        total = total + jnp.where(((rows >> sh) == (cols >> sh))[None], sc, 0.0)
        m *= 2
    return total


def _factor_block(seq_len):
    return min(FACTOR_BLOCK, seq_len)


def _block_deviation(b, seq_len):
    blk = _factor_block(seq_len)
    return b - _block_row(b, blk, blk // 2 - 1)


def _gated_intra(q, k, v, b, dev, dk, dv, n_heads, seq_len, factorise):
    rows = _iota((ROWS, ROWS), 0)
    cols = _iota((ROWS, ROWS), 1)
    if factorise:
        blk = _factor_block(seq_len)
        qf = _stack_heads(q * jnp.exp(dev), dk, n_heads)
        kf = (k * jnp.exp(-dev)).astype(BF16)
        sc = _dot_nt(qf, kf).reshape(n_heads, ROWS, ROWS)
        sh = _log2(blk)
        keep = ((rows >> sh) == (cols >> sh)) & (cols <= rows)
        total = jnp.where(keep[None], sc, 0.0) + _tree_levels(q, k, b, dk, n_heads, blk, seq_len)
    else:
        sc = _dot_nt(_stack_heads(q, dk, n_heads), k.astype(BF16)).reshape(n_heads, ROWS, ROWS)
        total = jnp.where((rows == cols)[None], sc, 0.0) + _tree_levels(q, k, b, dk, n_heads, 1, seq_len)
    return _apply_scores(total, v, dv, n_heads)


def _column(row_vec):
    c = row_vec.shape[1]
    return jnp.transpose(jnp.broadcast_to(row_vec, (8, c)))[:, 0:1]


def _gated_prompt(q, k, v, b, dev, dk, dv, n_heads, factorise, s_ref):
    o = _gated_intra(q, k, v, b, dev, dk, dv, n_heads, ROWS, factorise)
    s0 = s_ref[...]
    o = o + _dot((q * jnp.exp(b)).astype(BF16), s0.astype(BF16))
    b_last = b[ROWS - 1:ROWS, :]
    khat = (k * jnp.exp(b_last - b)).astype(BF16)
    ds = _dot_tn(khat, v.astype(BF16))
    keep = _head_grid_mask(q.shape[1], v.shape[1], dk, dv, n_heads)
    s_ref[...] = s0 * _column(jnp.exp(b_last)) + jnp.where(keep, ds, 0.0)
    return o


def _gated_sample(q, k, v, b, dev, dk, dv, n_heads, factorise, seq_len, s_ref, snew_ref):
    n_seq = ROWS // seq_len
    o = _gated_intra(q, k, v, b, dev, dk, dv, n_heads, seq_len, factorise)
    qhat = (q * jnp.exp(b)).astype(BF16)
    b_last = _block_row(b, seq_len, seq_len - 1)
    khat = (k * jnp.exp(b_last - b)).astype(BF16)
    decay_t = jnp.exp(jnp.transpose(b))
    v_bf = v.astype(BF16)
    o_inter = []
    for h in range(n_heads):
        s0 = s_ref[:, h]
        qh = qhat[:, dk * h:dk * (h + 1)].reshape(n_seq, seq_len, dk)
        oi = jnp.einsum('bqd,bdv->bqv', qh, s0.astype(BF16), preferred_element_type=F32)
        o_inter.append(oi.reshape(ROWS, dv))
        kh = khat[:, dk * h:dk * (h + 1)].reshape(n_seq, seq_len, dk)
        vh = v_bf[:, dv * h:dv * (h + 1)].reshape(n_seq, seq_len, dv)
        ds = jnp.einsum('bsd,bsv->bdv', kh, vh, preferred_element_type=F32)
        for s in range(n_seq):
            last = seq_len * s + seq_len - 1
            col = decay_t[dk * h:dk * (h + 1), last:last + 1]
            snew_ref[s, h] = col * s0[s] + ds[s]
    pad = v.shape[1] - n_heads * dv
    if pad:
        o_inter.append(jnp.zeros((ROWS, pad), F32))
    return o + jnp.concatenate(o_inter, axis=1)


_LOG_GAMMA = tuple(math.log(1.0 - 2.0 ** (-5.0 - h)) for h in range(B_HEADS))


def _per_head_lanes(width, dh, values):
    lane = _iota((1, width), 1)
    out = jnp.zeros((1, width), F32)
    for h, val in enumerate(values):
        out = jnp.where((lane >= dh * h) & (lane < dh * (h + 1)), val, out)
    return out


def _rotary(x, cos, sin_signed):
    w = x.shape[1]
    lane = _iota((1, w), 1)
    first_half = (lane & (B_DK - 1)) < (B_DK // 2)
    partner = jnp.where(first_half, pltpu.roll(x, w - B_DK // 2, 1), pltpu.roll(x, B_DK // 2, 1))
    return x * cos + partner * sin_signed


def _lane_groups(width):
    return [(lo, min(lo + MXU_DIM, width)) for lo in range(0, width, MXU_DIM)]


def _retention_intra(q, k, v, seq_len):
    rows = _iota((ROWS, ROWS), 0)
    cols = _iota((ROWS, ROWS), 1)
    sh = _log2(seq_len)
    causal = ((rows >> sh) == (cols >> sh)) & (cols <= rows)
    dist = (rows - cols).astype(F32)
    outs = []
    for lo, hi in _lane_groups(B_W):
        nh = (hi - lo) // B_DK
        sc = _dot_nt(_stack_heads(q[:, lo:hi], B_DK, nh), k[:, lo:hi].astype(BF16)).reshape(nh, ROWS, ROWS)
        gam = jnp.stack([jnp.where(causal, jnp.exp(dist * lg), 0.0)
                         for lg in _LOG_GAMMA[lo // B_DK:hi // B_DK]])
        outs.append(_apply_scores(sc * gam, v[:, lo:hi], B_DV, nh))
    return jnp.concatenate(outs, axis=1)


def _retention_scaled(q, k, seq_len):
    tau = (_iota((ROWS, 1), 0) & (seq_len - 1)).astype(F32)
    lg = _per_head_lanes(B_W, B_DK, _LOG_GAMMA)
    qhat = (q * jnp.exp(lg * (tau + 1.0))).astype(BF16)
    khat = (k * jnp.exp(lg * (seq_len - 1.0 - tau))).astype(BF16)
    return qhat, khat


def _retention_prompt(q, k, v, s_ref):
    o = _retention_intra(q, k, v, ROWS)
    qhat, khat = _retention_scaled(q, k, ROWS)
    decay = _column(_per_head_lanes(B_W, B_DK, [math.exp(ROWS * lg) for lg in _LOG_GAMMA]))
    v_bf = v.astype(BF16)
    o_inter = []
    for lo, hi in _lane_groups(B_W):
        s0 = s_ref[lo:hi, lo:hi]
        o_inter.append(_dot(qhat[:, lo:hi], s0.astype(BF16)))
        ds = _dot_tn(khat[:, lo:hi], v_bf[:, lo:hi])
        keep = _head_grid_mask(hi - lo, hi - lo, B_DK, B_DV, (hi - lo) // B_DK)
        s_ref[lo:hi, lo:hi] = s0 * decay[lo:hi, :] + jnp.where(keep, ds, 0.0)
    return o + jnp.concatenate(o_inter, axis=1)


def _retention_sample(q, k, v, seq_len, s_ref, snew_ref):
    n_seq = ROWS // seq_len
    o = _retention_intra(q, k, v, seq_len)
    qhat, khat = _retention_scaled(q, k, seq_len)
    v_bf = v.astype(BF16)
    o_inter = []
    for h in range(B_HEADS):
        s0 = s_ref[:, h]
        qh = qhat[:, B_DK * h:B_DK * (h + 1)].reshape(n_seq, seq_len, B_DK)
        oi = jnp.einsum('bqd,bdv->bqv', qh, s0.astype(BF16), preferred_element_type=F32)
        o_inter.append(oi.reshape(ROWS, B_DV))
        kh = khat[:, B_DK * h:B_DK * (h + 1)].reshape(n_seq, seq_len, B_DK)
        vh = v_bf[:, B_DV * h:B_DV * (h + 1)].reshape(n_seq, seq_len, B_DV)
        ds = jnp.einsum('bsd,bsv->bdv', kh, vh, preferred_element_type=F32)
        snew_ref[:, h] = s0 * math.exp(seq_len * _LOG_GAMMA[h]) + ds
    return o + jnp.concatenate(o_inter, axis=1)


def _lower_bound(lbl_ref, layer):
    rows = [lbl_ref[i:i + 1, :] for i in range(DEPTH)]
    mx = functools.reduce(jnp.maximum, rows)
    ex = [jnp.exp(r - mx) for r in rows]
    tot = functools.reduce(lambda a, c: a + c, ex)
    acc = functools.reduce(lambda a, c: a + c, ex[:layer + 1]) / tot
    return acc - ex[0] / tot


def _mix_gates(proj_ref, lbl_ref, w2_ref, gb_ref, layer, seq_len):
    seg = lambda off, w: proj_ref[:, off:off + w]
    af = seg(O_AF, A_W)
    if layer == 0:
        log_f = jax.nn.log_sigmoid(af)
        k_a = jax.nn.sigmoid(-af)
    else:
        lb = _lower_bound(lbl_ref, layer)
        log_f = jnp.logaddexp(jnp.log(lb), jnp.log1p(-lb) + jax.nn.log_sigmoid(af))
        k_a = (1.0 - lb) * jax.nn.sigmoid(-af)
    gk = _dot(seg(O_CR, C_RP).astype(BF16), w2_ref[...]) + gb_ref[...]
    log_a = jax.nn.log_sigmoid(gk) * (1.0 / C_TAU)
    b_a = _seq_cumsum(log_f, seq_len)
    b_c = _seq_cumsum(log_a, seq_len)
    dev_a = _block_deviation(b_a, seq_len)
    dev_c = _block_deviation(b_c, seq_len)
    worst = jnp.maximum(jnp.max(jnp.abs(dev_a)), jnp.max(jnp.abs(dev_c)))
    return (k_a, b_a, dev_a, b_c, dev_c), worst


def _mix_rest(proj_ref, gates, cos_ref, sin_ref, hn_ref, gn_ref, gated, retention, factorise):
    seg = lambda off, w: proj_ref[:, off:off + w]
    k_a, b_a, dev_a, b_c, dev_c = gates
    o_a = gated("a", _silu(seg(O_AQ, A_W)), k_a, seg(O_AI, A_W), b_a, dev_a, A_DK, A_DV, A_HEADS, factorise)
    y_a = _head_norm(o_a, A_DV, A_HEADS) * hn_ref[...] * _silu(seg(O_AG, A_W))
    q_b = _rotary(seg(O_BQ, B_W), cos_ref[...], sin_ref[...])
    k_b = _rotary(seg(O_BK, B_W), cos_ref[...], sin_ref[...]) * (B_DK ** -0.5)
    o_b = retention(q_b, k_b, seg(O_BV, B_W))
    y_b = _head_norm(o_b, B_DV, B_HEADS) * _silu(seg(O_BG, B_W))
    o_c = gated("c", seg(O_CQ, C_QP) * (C_DK ** -0.5), seg(O_CK, C_QP), seg(O_CV, C_VW), b_c, dev_c,
                C_DK, C_DV, C_HEADS, factorise)
    y_c = _head_norm(o_c, C_DV, C_HEADS) * gn_ref[...] * _silu(seg(O_CG, C_VW))
    return jnp.concatenate([y_a, y_b, y_c], axis=1).astype(BF16)


def _when_safe(worst, body):
    safe = worst < FACTOR_MAX_EXP

    @pl.when(safe)
    def _():
        body(True)

    @pl.when(jnp.logical_not(safe))
    def _():
        body(False)


def _mix_prompt_kernel(proj_ref, cos_ref, sin_ref, lbl_ref, hn_ref, w2_ref, gb_ref, gn_ref,
                       y_ref, sa_ref, sb_ref, sc_ref, sa_acc, sb_acc, sc_acc, *, layer):
    t = pl.program_id(1)

    @pl.when(t == 0)
    def _():
        sa_acc[...] = jnp.zeros_like(sa_acc)
        sb_acc[...] = jnp.zeros_like(sb_acc)
        sc_acc[...] = jnp.zeros_like(sc_acc)

    n_par = proj_ref.shape[0]
    gates, worst = [], None
    for i in range(n_par):
        g, w = _mix_gates(proj_ref.at[i], lbl_ref, w2_ref, gb_ref, layer, ROWS)
        gates.append(g)
        worst = w if worst is None else jnp.maximum(worst, w)

    def body(factorise):
        for i in range(n_par):
            def gated(which, q, k, v, b, dev, dk, dv, n_heads, fac):
                acc = sa_acc if which == "a" else sc_acc
                return _gated_prompt(q, k, v, b, dev, dk, dv, n_heads, fac, acc.at[i])

            def retention(q, k, v):
                return _retention_prompt(q, k, v, sb_acc.at[i])

            y_ref[i] = _mix_rest(proj_ref.at[i], gates[i], cos_ref, sin_ref, hn_ref, gn_ref, gated, retention,
                                 factorise)

    _when_safe(worst, body)

    @pl.when(t == pl.num_programs(1) - 1)
    def _():
        for i in range(n_par):
            for h in range(A_HEADS):
                sa_ref[i, h] = sa_acc[i, A_DK * h:A_DK * (h + 1), A_DV * h:A_DV * (h + 1)]
            for h in range(B_HEADS):
                sb_ref[i, h] = sb_acc[i, B_DK * h:B_DK * (h + 1), B_DV * h:B_DV * (h + 1)]
            for h in range(C_HEADS):
                sc_ref[i, h] = sc_acc[i, C_DK * h:C_DK * (h + 1), C_DV * h:C_DV * (h + 1)]


def _mix_sample_kernel(proj_ref, cos_ref, sin_ref, lbl_ref, hn_ref, w2_ref, gb_ref, gn_ref,
                       sa_ref, sb_ref, sc_ref, y_ref, sa_new, sb_new, sc_new, *, layer, seq_len):
    gates, worst = _mix_gates(proj_ref, lbl_ref, w2_ref, gb_ref, layer, seq_len)

    def gated(which, q, k, v, b, dev, dk, dv, n_heads, fac):
        s_ref, snew_ref = (sa_ref, sa_new) if which == "a" else (sc_ref, sc_new)
        return _gated_sample(q, k, v, b, dev, dk, dv, n_heads, fac, seq_len, s_ref, snew_ref)

    def retention(q, k, v):
        return _retention_sample(q, k, v, seq_len, sb_ref, sb_new)

    def body(factorise):
        y_ref[...] = _mix_rest(proj_ref, gates, cos_ref, sin_ref, hn_ref, gn_ref, gated, retention, factorise)

    _when_safe(worst, body)


def _const(shape):
    return pl.BlockSpec(shape, lambda *_: (0,) * len(shape))


def _mix_prompt(proj, cos, sin, lbl, hn, w2, gb, gn, layer):
    bsz, seq, _ = proj.shape
    n_par = SEQS_PER_STEP if bsz % SEQS_PER_STEP == 0 else 1
    tile = lambda w: pl.BlockSpec((n_par, ROWS, w), lambda b, t: (b, t, 0))
    tab = pl.BlockSpec((ROWS, B_W), lambda b, t: (t, 0))
    st = lambda h, dk, dv: pl.BlockSpec((n_par, h, dk, dv), lambda b, t: (b, 0, 0, 0))
    return pl.pallas_call(
        functools.partial(_mix_prompt_kernel, layer=layer),
        grid=(bsz // n_par, seq // ROWS),
        in_specs=[tile(NP), tab, tab, _const((DEPTH, A_W)), _const((1, A_W)), _const((C_RP, C_QP)),
                  _const((1, C_QP)), _const((1, C_VW))],
        out_specs=[tile(D_MODEL), st(A_HEADS, A_DK, A_DV), st(B_HEADS, B_DK, B_DV), st(C_HEADS, C_DK, C_DV)],
        out_shape=[jax.ShapeDtypeStruct((bsz, seq, D_MODEL), BF16),
                   jax.ShapeDtypeStruct((bsz, A_HEADS, A_DK, A_DV), F32),
                   jax.ShapeDtypeStruct((bsz, B_HEADS, B_DK, B_DV), F32),
                   jax.ShapeDtypeStruct((bsz, C_HEADS, C_DK, C_DV), F32)],
        scratch_shapes=[pltpu.VMEM((n_par, A_W, A_W), F32), pltpu.VMEM((n_par, B_W, B_W), F32),
                        pltpu.VMEM((n_par, C_QP, C_VW), F32)],
        compiler_params=pltpu.CompilerParams(dimension_semantics=("parallel", "arbitrary"),
                                             vmem_limit_bytes=VMEM_LIMIT),
        name="mix_prompt",
    )(proj, cos, sin, lbl, hn, w2, gb, gn)


def _mix_sample(proj, cos, sin, lbl, hn, w2, gb, gn, sa, sb, sc, layer, seq_len):
    m = proj.shape[0]
    n_seq = ROWS // seq_len
    tile = lambda w: pl.BlockSpec((ROWS, w), lambda i: (i, 0))
    st_in = lambda h, dk, dv: pl.BlockSpec((None, n_seq, h, dk, dv), lambda i: (layer, i, 0, 0, 0))
    st_out = lambda h, dk, dv: pl.BlockSpec((n_seq, h, dk, dv), lambda i: (i, 0, 0, 0))
    dims = [(A_HEADS, A_DK, A_DV), (B_HEADS, B_DK, B_DV), (C_HEADS, C_DK, C_DV)]
    return pl.pallas_call(
        functools.partial(_mix_sample_kernel, layer=layer, seq_len=seq_len),
        grid=(m // ROWS,),
        in_specs=[tile(NP), _const((ROWS, B_W)), _const((ROWS, B_W)), _const((DEPTH, A_W)), _const((1, A_W)),
                  _const((C_RP, C_QP)), _const((1, C_QP)), _const((1, C_VW))] + [st_in(*d) for d in dims],
        out_specs=[tile(D_MODEL)] + [st_out(*d) for d in dims],
        out_shape=[jax.ShapeDtypeStruct((m, D_MODEL), BF16), jax.ShapeDtypeStruct(sa.shape[1:], F32),
                   jax.ShapeDtypeStruct(sb.shape[1:], F32), jax.ShapeDtypeStruct(sc.shape[1:], F32)],
        compiler_params=pltpu.CompilerParams(dimension_semantics=("parallel",), vmem_limit_bytes=VMEM_LIMIT),
        name="mix_sample",
    )(proj, cos, sin, lbl, hn, w2, gb, gn, sa, sb, sc)


def _rope_tables(pos):
    half = B_DK // 2
    inv_freq = ROPE_BASE ** (-jnp.arange(half, dtype=F32) / half)
    ang = pos[:, None] * inv_freq[None, :]
    cos, sin = jnp.cos(ang), jnp.sin(ang)
    cos_h = jnp.concatenate([cos, cos], axis=1)
    sin_h = jnp.concatenate([-sin, sin], axis=1)
    return jnp.tile(cos_h, (1, B_HEADS)), jnp.tile(sin_h, (1, B_HEADS))


def _pad_w_in(w):
    z = lambda n: jnp.zeros(w.shape[:2] + (n,), w.dtype)
    c0 = 2560
    return jnp.concatenate([
        w[..., :c0],
        w[..., c0:c0 + C_QW], z(C_QP - C_QW),
        w[..., c0 + C_QW:c0 + 2 * C_QW], z(C_QP - C_QW),
        w[..., c0 + 2 * C_QW:c0 + 2 * C_QW + 2 * C_VW],
        w[..., c0 + 2 * C_QW + 2 * C_VW:], z(C_RP - C_RANK)], axis=-1)


def kernel(x_prompt, x_sample, state_hgrn, state_ret, state_gla, ffn1_norm, ffn1_w_in, ffn1_w_out, mix_norm, w_in, hgrn_lb_logits, hgrn_norm, gla_w2, gla_b, gla_norm, w_out, ffn2_norm, ffn2_w_in, ffn2_w_out, final_norm):
    bsz, seq, _ = x_prompt.shape
    dbs, dseq, _ = x_sample.shape
    assert seq % ROWS == 0 and ROWS % dseq == 0 and (bsz * seq) % TM == 0 and (dbs * dseq) % TM == 0

    row = lambda a: a.reshape(1, -1).astype(F32)
    f1_in, f1_out = ffn1_w_in.astype(BF16), ffn1_w_out.astype(BF16)
    f2_in, f2_out = ffn2_w_in.astype(BF16), ffn2_w_out.astype(BF16)
    wo = w_out.astype(BF16)
    wmix = _pad_w_in(w_in.astype(BF16))
    stacked = lambda a: a.reshape(DEPTH, 1, -1).astype(F32)
    n1, nm, n2 = stacked(ffn1_norm), stacked(mix_norm), stacked(ffn2_norm)
    w2 = [jnp.zeros((C_RP, C_QP), BF16).at[:C_RANK, :C_QW].set(gla_w2[l].astype(BF16)) for l in range(DEPTH)]
    gb = [jnp.zeros((1, C_QP), F32).at[0, :C_QW].set(gla_b[l]) for l in range(DEPTH)]
    hn = [row(jnp.tile(hgrn_norm[l], A_HEADS)) for l in range(DEPTH)]
    gn = [row(jnp.tile(gla_norm[l], C_HEADS)) for l in range(DEPTH)]
    lbl = hgrn_lb_logits.astype(F32)

    cos_p, sin_p = _rope_tables(jnp.arange(seq, dtype=F32))
    cos_s, sin_s = _rope_tables(PAST_LEN + jnp.arange(dseq, dtype=F32))
    cos_s, sin_s = jnp.tile(cos_s, (ROWS // dseq, 1)), jnp.tile(sin_s, (ROWS // dseq, 1))

    def layer_tail(x1, y, l):
        return _out_ffn(x1, y, wo, n2, f2_in, f2_out, row(final_norm), l)

    x = x_prompt.reshape(bsz * seq, D_MODEL)
    p_states = []
    for l in range(DEPTH):
        x1, proj = _ffn_proj(x, n1, f1_in, f1_out, nm, wmix, l)
        y, sa, sb, sc = _mix_prompt(proj.reshape(bsz, seq, NP), cos_p, sin_p, lbl, hn[l], w2[l], gb[l], gn[l], l)
        x = layer_tail(x1, y.reshape(bsz * seq, D_MODEL), l)
        p_states.append((sa, sb, sc))
    y_prompt = x.reshape(bsz, seq, D_MODEL)

    x = x_sample.reshape(dbs * dseq, D_MODEL)
    s_states = []
    for l in range(DEPTH):
        x1, proj = _ffn_proj(x, n1, f1_in, f1_out, nm, wmix, l)
        y, sa, sb, sc = _mix_sample(proj, cos_s, sin_s, lbl, hn[l], w2[l], gb[l], gn[l],
                                    state_hgrn, state_ret, state_gla, l, dseq)
        x = layer_tail(x1, y, l)
        s_states.append((sa, sb, sc))
    y_sample = x.reshape(dbs, dseq, D_MODEL)

    stack = lambda states, i: jnp.stack([s[i] for s in states])
    return (y_prompt, y_sample, stack(p_states, 0), stack(p_states, 1), stack(p_states, 2),
            stack(s_states, 0), stack(s_states, 1), stack(s_states, 2))
```

```python
import functools
import math

import jax
import jax.numpy as jnp
from jax import lax
from jax.experimental import pallas as pl
from jax.experimental.pallas import tpu as pltpu

F32, BF16 = jnp.float32, jnp.bfloat16

D_MODEL = 1024
DEPTH = 2
A_HEADS, A_DK, A_DV = 4, 64, 64
B_HEADS, B_DK, B_DV = 6, 64, 64
C_HEADS, C_DK, C_DV = 4, 48, 96
C_RANK = 16
C_TAU = 16.0
D_FF = 2816
ROPE_BASE = 10000.0
EPS = 1e-6
PAST_LEN = 16384

A_W = A_HEADS * A_DK
B_W = B_HEADS * B_DK
C_QW = C_HEADS * C_DK
C_VW = C_HEADS * C_DV
LANES = 128
SUBLANES = 8
MXU_DIM = 256
C_QP = 256
C_RP = LANES

O_AQ, O_AF, O_AI, O_AG = 0, 256, 512, 768
O_BQ, O_BK, O_BV, O_BG = 1024, 1408, 1792, 2176
O_CQ, O_CK, O_CV, O_CG, O_CR = 2560, 2816, 3072, 3456, 3840
NP = O_CR + C_RP

ROWS = 128
SEQS_PER_STEP = 2
FACTOR_BLOCK = 64
FACTOR_MAX_EXP = 80.0
FF_CHUNK = 256
TM = 256
VMEM_LIMIT = 56 * 1024 * 1024


def _dot(a, b):
    return jnp.dot(a, b, preferred_element_type=F32)


def _dot_nt(a, b):
    return lax.dot_general(a, b, (((1,), (1,)), ((), ())), preferred_element_type=F32)


def _dot_tn(a, b):
    return lax.dot_general(a, b, (((0,), (0,)), ((), ())), preferred_element_type=F32)


def _iota(shape, dim):
    return lax.broadcasted_iota(jnp.int32, shape, dim)


def _split3(x):
    hi = x.astype(BF16)
    r = x - hi.astype(F32)
    mid = r.astype(BF16)
    lo = (r - mid.astype(F32)).astype(BF16)
    return hi, mid, lo


def _dot01(mat01, x):
    hi, mid, lo = _split3(x)
    return _dot(mat01, hi) + _dot(mat01, mid) + _dot(mat01, lo)


def _rms(x, w):
    ms = jnp.mean(x * x, axis=-1, keepdims=True)
    return x * lax.rsqrt(ms + EPS) * w


def _silu(x):
    return x * jax.nn.sigmoid(x)


def _swiglu(h_bf, win_ref, wout_ref, act_ref):
    for c in range(D_FF // FF_CHUNK):
        lo = c * FF_CHUNK
        g = _dot(h_bf, win_ref[:, lo:lo + FF_CHUNK])
        u = _dot(h_bf, win_ref[:, D_FF + lo:D_FF + lo + FF_CHUNK])
        act_ref[:, lo:lo + FF_CHUNK] = (_silu(g) * u).astype(BF16)
    return _dot(act_ref[...], wout_ref[...])


def _lower_bound(logit_rows, layer):
    mx = functools.reduce(jnp.maximum, logit_rows)
    ex = [jnp.exp(r - mx) for r in logit_rows]
    tot = functools.reduce(lambda a, c: a + c, ex)
    acc = functools.reduce(lambda a, c: a + c, ex[:layer + 1]) / tot
    return acc - ex[0] / tot


def _hgrn_gate(af, logit_rows, layer):
    if layer == 0:
        return jax.nn.log_sigmoid(af), jax.nn.sigmoid(-af)
    lb = _lower_bound(logit_rows, layer)
    log_f = jnp.logaddexp(jnp.log(lb), jnp.log1p(-lb) + jax.nn.log_sigmoid(af))
    return log_f, (1.0 - lb) * jax.nn.sigmoid(-af)


def _ffn_proj_kernel(x_ref, n1_ref, win_ref, wout_ref, nm_ref, wmix_ref, x1_ref, proj_ref, act_ref, *,
                     channel_major):
    x = x_ref[...]
    h = _rms(x, n1_ref[...]).astype(BF16)
    x1 = x + 0.5 * _swiglu(h, win_ref, wout_ref, act_ref)
    x1_ref[...] = x1
    hm = _rms(x1, nm_ref[...]).astype(BF16)
    proj_ref[...] = _dot_nt(wmix_ref[...], hm) if channel_major else _dot(hm, wmix_ref[...])


def _out_ffn_kernel(x_ref, *refs, final, channel_major):
    n_y = 3 if channel_major else 1
    y_refs = refs[:n_y]
    wo_ref, n2_ref, win_ref, wout_ref, nf_ref, o_ref, act_ref = refs[n_y:]
    x2 = x_ref[...]
    if channel_major:
        lo = 0
        for y_ref in y_refs:
            w = y_ref.shape[0]
            x2 = x2 + _dot_tn(y_ref[...], wo_ref[lo:lo + w, :])
            lo += w
    else:
        x2 = x2 + _dot(y_refs[0][...], wo_ref[...])
    h = _rms(x2, n2_ref[...]).astype(BF16)
    x3 = x2 + 0.5 * _swiglu(h, win_ref, wout_ref, act_ref)
    if final:
        x3 = _rms(x3, nf_ref[...])
    o_ref[...] = x3


def _resident(shape, layer=None):
    if layer is None:
        return pl.BlockSpec(shape, lambda i: (0,) * len(shape), pipeline_mode=pl.Buffered(1))
    return pl.BlockSpec((None,) + shape, lambda i: (layer,) + (0,) * len(shape), pipeline_mode=pl.Buffered(1))


def _row_tile(width):
    return pl.BlockSpec((TM, width), lambda i: (i, 0))


def _col_tile(height):
    return pl.BlockSpec((height, TM), lambda i: (0, i))


def _ffn_proj(x, n1, win, wout, nm, wmix, layer, channel_major):
    m = x.shape[0]
    wmix_shape = (NP, D_MODEL) if channel_major else (D_MODEL, NP)
    return pl.pallas_call(
        functools.partial(_ffn_proj_kernel, channel_major=channel_major),
        grid=(m // TM,),
        in_specs=[_row_tile(D_MODEL), _resident((1, D_MODEL), layer), _resident((D_MODEL, 2 * D_FF), layer),
                  _resident((D_FF, D_MODEL), layer), _resident((1, D_MODEL), layer), _resident(wmix_shape, layer)],
        out_specs=[_row_tile(D_MODEL), _col_tile(NP) if channel_major else _row_tile(NP)],
        out_shape=[jax.ShapeDtypeStruct((m, D_MODEL), F32),
                   jax.ShapeDtypeStruct((NP, m) if channel_major else (m, NP), F32)],
        scratch_shapes=[pltpu.VMEM((TM, D_FF), BF16)],
        compiler_params=pltpu.CompilerParams(dimension_semantics=("parallel",), vmem_limit_bytes=VMEM_LIMIT),
        name="ffn1_proj",
    )(x, n1, win, wout, nm, wmix)


def _out_ffn(x, ys, wo, n2, win, wout, nf, layer, channel_major):
    m = x.shape[0]
    y_specs = [_col_tile(y.shape[0]) for y in ys] if channel_major else [_row_tile(D_MODEL)]
    return pl.pallas_call(
        functools.partial(_out_ffn_kernel, final=layer == DEPTH - 1, channel_major=channel_major),
        grid=(m // TM,),
        in_specs=[_row_tile(D_MODEL)] + y_specs + [
            _resident((D_MODEL, D_MODEL), layer), _resident((1, D_MODEL), layer),
            _resident((D_MODEL, 2 * D_FF), layer), _resident((D_FF, D_MODEL), layer), _resident((1, D_MODEL))],
        out_specs=_row_tile(D_MODEL),
        out_shape=jax.ShapeDtypeStruct((m, D_MODEL), F32),
        scratch_shapes=[pltpu.VMEM((TM, D_FF), BF16)],
        compiler_params=pltpu.CompilerParams(dimension_semantics=("parallel",), vmem_limit_bytes=VMEM_LIMIT),
        name="outproj_ffn2",
    )(x, *ys, wo, n2, win, wout, nf)


def _log2(n):
    assert n & (n - 1) == 0
    return n.bit_length() - 1


def _head_mask(width, dh, h):
    lane = _iota((1, width), 1)
    return (lane >= dh * h) & (lane < dh * (h + 1))


def _head_grid_mask(rows, cols, dr, dc, n_heads):
    r = _iota((rows, cols), 0)
    c = _iota((rows, cols), 1)
    m = None
    for h in range(n_heads):
        mh = (r >= dr * h) & (r < dr * (h + 1)) & (c >= dc * h) & (c < dc * (h + 1))
        m = mh if m is None else (m | mh)
    return m


def _seg_mean_sq(o, dh, n_heads):
    w = o.shape[1]
    if w > MXU_DIM and MXU_DIM % dh == 0:
        parts = [_seg_mean_sq(o[:, lo:min(lo + MXU_DIM, w)], dh, (min(lo + MXU_DIM, w) - lo) // dh)
                 for lo in range(0, w, MXU_DIM)]
        return jnp.concatenate(parts, axis=1)
    ones = jnp.where(_head_grid_mask(w, w, dh, dh, n_heads), 1.0, 0.0).astype(BF16)
    sq = o * o
    hi = sq.astype(BF16)
    lo = (sq - hi.astype(F32)).astype(BF16)
    return (_dot(hi, ones) + _dot(lo, ones)) * (1.0 / dh)


def _head_norm(o, dh, n_heads):
    return o * lax.rsqrt(_seg_mean_sq(o, dh, n_heads) + EPS)


def _tile_cumsum(g):
    rows = _iota((ROWS, ROWS), 0)
    cols = _iota((ROWS, ROWS), 1)
    return _dot01(jnp.where(cols <= rows, 1.0, 0.0).astype(BF16), g)


def _block_row(b, size, idx):
    c = b.shape[1]
    if size >= SUBLANES:
        b3 = b.reshape(ROWS // size, size, c)
        return jnp.broadcast_to(b3[:, idx:idx + 1, :], b3.shape).reshape(ROWS, c)
    b8 = b.reshape(ROWS // SUBLANES, SUBLANES, c)
    sub = _iota(b8.shape, 1)
    ref = b8[:, idx:idx + 1, :]
    for blk in range(1, SUBLANES // size):
        lo = blk * size
        ref = jnp.where(sub >= lo, b8[:, lo + idx:lo + idx + 1, :], ref)
    return jnp.broadcast_to(ref, b8.shape).reshape(ROWS, c)


def _stack_heads(x, dh, n_heads, extra_mask=None):
    parts = []
    for h in range(n_heads):
        m = _head_mask(x.shape[1], dh, h)
        if extra_mask is not None:
            m = m & extra_mask
        parts.append(jnp.where(m, x, 0.0))
    return jnp.concatenate(parts, axis=0).astype(BF16)


def _apply_scores(p, v, dv, n_heads):
    pcat = jnp.concatenate([p[h].astype(BF16) for h in range(n_heads)], axis=1)
    return _dot(pcat, _stack_heads(v, dv, n_heads))


def _tree_levels(q, k, b, dk, n_heads, first):
    rows = _iota((ROWS, ROWS), 0)
    cols = _iota((ROWS, ROWS), 1)
    rowc = _iota((ROWS, q.shape[1]), 0)
    total = jnp.zeros((n_heads, ROWS, ROWS), F32)
    m = first
    while m < ROWS:
        ref = _block_row(b, 2 * m, m - 1)
        up = (rowc & (2 * m - 1)) >= m
        e = jnp.exp(jnp.where(up, b - ref, ref - b))
        ql = _stack_heads(q * e, dk, n_heads, extra_mask=up)
        kl = jnp.where(up, 0.0, k * e).astype(BF16)
        sc = _dot_nt(ql, kl).reshape(n_heads, ROWS, ROWS)
        sh = _log2(2 * m)
        total = total + jnp.where(((rows >> sh) == (cols >> sh))[None], sc, 0.0)
        m *= 2
    return total


def _block_deviation(b):
    return b - _block_row(b, FACTOR_BLOCK, FACTOR_BLOCK // 2 - 1)


def _gated_intra(q, k, v, b, dev, dk, dv, n_heads, factorise):
    rows = _iota((ROWS, ROWS), 0)
    cols = _iota((ROWS, ROWS), 1)
    if factorise:
        qf = _stack_heads(q * jnp.exp(dev), dk, n_heads)
        kf = (k * jnp.exp(-dev)).astype(BF16)
        sc = _dot_nt(qf, kf).reshape(n_heads, ROWS, ROWS)
        sh = _log2(FACTOR_BLOCK)
        keep = ((rows >> sh) == (cols >> sh)) & (cols <= rows)
        total = jnp.where(keep[None], sc, 0.0) + _tree_levels(q, k, b, dk, n_heads, FACTOR_BLOCK)
    else:
        sc = _dot_nt(_stack_heads(q, dk, n_heads), k.astype(BF16)).reshape(n_heads, ROWS, ROWS)
        total = jnp.where((rows == cols)[None], sc, 0.0) + _tree_levels(q, k, b, dk, n_heads, 1)
    return _apply_scores(total, v, dv, n_heads)


def _column(row_vec):
    c = row_vec.shape[1]
    return jnp.transpose(jnp.broadcast_to(row_vec, (SUBLANES, c)))[:, 0:1]


def _gated_tile(q, k, v, b, dev, dk, dv, n_heads, factorise, s_ref):
    o = _gated_intra(q, k, v, b, dev, dk, dv, n_heads, factorise)
    s0 = s_ref[...]
    o = o + _dot((q * jnp.exp(b)).astype(BF16), s0.astype(BF16))
    b_last = b[ROWS - 1:ROWS, :]
    khat = (k * jnp.exp(b_last - b)).astype(BF16)
    ds = _dot_tn(khat, v.astype(BF16))
    keep = _head_grid_mask(q.shape[1], v.shape[1], dk, dv, n_heads)
    s_ref[...] = s0 * _column(jnp.exp(b_last)) + jnp.where(keep, ds, 0.0)
    return o


_LOG_GAMMA = tuple(math.log(1.0 - 2.0 ** (-5.0 - h)) for h in range(B_HEADS))


def _per_head_lanes(width, dh, values):
    lane = _iota((1, width), 1)
    out = jnp.zeros((1, width), F32)
    for h, val in enumerate(values):
        out = jnp.where((lane >= dh * h) & (lane < dh * (h + 1)), val, out)
    return out


def _rotary(x, cos, sin_signed):
    w = x.shape[1]
    lane = _iota((1, w), 1)
    first_half = (lane & (B_DK - 1)) < (B_DK // 2)
    partner = jnp.where(first_half, pltpu.roll(x, w - B_DK // 2, 1), pltpu.roll(x, B_DK // 2, 1))
    return x * cos + partner * sin_signed


def _lane_groups(width):
    return [(lo, min(lo + MXU_DIM, width)) for lo in range(0, width, MXU_DIM)]


def _retention_tile(q, k, v, s_ref):
    rows = _iota((ROWS, ROWS), 0)
    cols = _iota((ROWS, ROWS), 1)
    causal = cols <= rows
    dist = (rows - cols).astype(F32)
    tau = _iota((ROWS, 1), 0).astype(F32)
    lg = _per_head_lanes(B_W, B_DK, _LOG_GAMMA)
    qhat = (q * jnp.exp(lg * (tau + 1.0))).astype(BF16)
    khat = (k * jnp.exp(lg * (ROWS - 1.0 - tau))).astype(BF16)
    decay = _column(_per_head_lanes(B_W, B_DK, [math.exp(ROWS * g) for g in _LOG_GAMMA]))
    v_bf = v.astype(BF16)
    outs = []
    for lo, hi in _lane_groups(B_W):
        nh = (hi - lo) // B_DK
        sc = _dot_nt(_stack_heads(q[:, lo:hi], B_DK, nh), k[:, lo:hi].astype(BF16)).reshape(nh, ROWS, ROWS)
        gam = jnp.stack([jnp.where(causal, jnp.exp(dist * g), 0.0) for g in _LOG_GAMMA[lo // B_DK:hi // B_DK]])
        s0 = s_ref[lo:hi, lo:hi]
        outs.append(_apply_scores(sc * gam, v[:, lo:hi], B_DV, nh) + _dot(qhat[:, lo:hi], s0.astype(BF16)))
        ds = _dot_tn(khat[:, lo:hi], v_bf[:, lo:hi])
        keep = _head_grid_mask(hi - lo, hi - lo, B_DK, B_DV, nh)
        s_ref[lo:hi, lo:hi] = s0 * decay[lo:hi, :] + jnp.where(keep, ds, 0.0)
    return jnp.concatenate(outs, axis=1)


def _mix_gates(proj_ref, lbl_ref, w2_ref, gb_ref, layer):
    seg = lambda off, w: proj_ref[:, off:off + w]
    log_f, k_a = _hgrn_gate(seg(O_AF, A_W), [lbl_ref[i:i + 1, :] for i in range(DEPTH)], layer)
    gk = _dot(seg(O_CR, C_RP).astype(BF16), w2_ref[...]) + gb_ref[...]
    log_a = jax.nn.log_sigmoid(gk) * (1.0 / C_TAU)
    b_a = _tile_cumsum(log_f)
    b_c = _tile_cumsum(log_a)
    dev_a = _block_deviation(b_a)
    dev_c = _block_deviation(b_c)
    worst = jnp.maximum(jnp.max(jnp.abs(dev_a)), jnp.max(jnp.abs(dev_c)))
    return (k_a, b_a, dev_a, b_c, dev_c), worst


def _mix_rest(proj_ref, gates, cos_ref, sin_ref, hn_ref, gn_ref, sa_ref, sb_ref, sc_ref, factorise):
    seg = lambda off, w: proj_ref[:, off:off + w]
    k_a, b_a, dev_a, b_c, dev_c = gates
    o_a = _gated_tile(_silu(seg(O_AQ, A_W)), k_a, seg(O_AI, A_W), b_a, dev_a, A_DK, A_DV, A_HEADS, factorise,
                      sa_ref)
    y_a = _head_norm(o_a, A_DV, A_HEADS) * hn_ref[...] * _silu(seg(O_AG, A_W))
    q_b = _rotary(seg(O_BQ, B_W), cos_ref[...], sin_ref[...])
    k_b = _rotary(seg(O_BK, B_W), cos_ref[...], sin_ref[...]) * (B_DK ** -0.5)
    o_b = _retention_tile(q_b, k_b, seg(O_BV, B_W), sb_ref)
    y_b = _head_norm(o_b, B_DV, B_HEADS) * _silu(seg(O_BG, B_W))
    o_c = _gated_tile(seg(O_CQ, C_QP) * (C_DK ** -0.5), seg(O_CK, C_QP), seg(O_CV, C_VW), b_c, dev_c,
                      C_DK, C_DV, C_HEADS, factorise, sc_ref)
    y_c = _head_norm(o_c, C_DV, C_HEADS) * gn_ref[...] * _silu(seg(O_CG, C_VW))
    return jnp.concatenate([y_a, y_b, y_c], axis=1).astype(BF16)


def _mix_prompt_kernel(proj_ref, cos_ref, sin_ref, lbl_ref, hn_ref, w2_ref, gb_ref, gn_ref,
                       y_ref, sa_ref, sb_ref, sc_ref, sa_acc, sb_acc, sc_acc, *, layer):
    t = pl.program_id(1)

    @pl.when(t == 0)
    def _():
        sa_acc[...] = jnp.zeros_like(sa_acc)
        sb_acc[...] = jnp.zeros_like(sb_acc)
        sc_acc[...] = jnp.zeros_like(sc_acc)

    n_par = proj_ref.shape[0]
    gates, worst = [], None
    for i in range(n_par):
        g, w = _mix_gates(proj_ref.at[i], lbl_ref, w2_ref, gb_ref, layer)
        gates.append(g)
        worst = w if worst is None else jnp.maximum(worst, w)

    def body(factorise):
        for i in range(n_par):
            y_ref[i] = _mix_rest(proj_ref.at[i], gates[i], cos_ref, sin_ref, hn_ref, gn_ref,
                                 sa_acc.at[i], sb_acc.at[i], sc_acc.at[i], factorise)

    safe = worst < FACTOR_MAX_EXP

    @pl.when(safe)
    def _():
        body(True)

    @pl.when(jnp.logical_not(safe))
    def _():
        body(False)

    @pl.when(t == pl.num_programs(1) - 1)
    def _():
        for i in range(n_par):
            for h in range(A_HEADS):
                sa_ref[i, h] = sa_acc[i, A_DK * h:A_DK * (h + 1), A_DV * h:A_DV * (h + 1)]
            for h in range(B_HEADS):
                sb_ref[i, h] = sb_acc[i, B_DK * h:B_DK * (h + 1), B_DV * h:B_DV * (h + 1)]
            for h in range(C_HEADS):
                sc_ref[i, h] = sc_acc[i, C_DK * h:C_DK * (h + 1), C_DV * h:C_DV * (h + 1)]


def _const(shape):
    return pl.BlockSpec(shape, lambda *_: (0,) * len(shape))


def _mix_prompt(proj, cos, sin, lbl, hn, w2, gb, gn, layer):
    bsz, seq, _ = proj.shape
    n_par = SEQS_PER_STEP if bsz % SEQS_PER_STEP == 0 else 1
    tile = lambda w: pl.BlockSpec((n_par, ROWS, w), lambda b, t: (b, t, 0))
    tab = pl.BlockSpec((ROWS, B_W), lambda b, t: (t, 0))
    st = lambda h, dk, dv: pl.BlockSpec((n_par, h, dk, dv), lambda b, t: (b, 0, 0, 0))
    return pl.pallas_call(
        functools.partial(_mix_prompt_kernel, layer=layer),
        grid=(bsz // n_par, seq // ROWS),
        in_specs=[tile(NP), tab, tab, _const((DEPTH, A_W)), _const((1, A_W)), _const((C_RP, C_QP)),
                  _const((1, C_QP)), _const((1, C_VW))],
        out_specs=[tile(D_MODEL), st(A_HEADS, A_DK, A_DV), st(B_HEADS, B_DK, B_DV), st(C_HEADS, C_DK, C_DV)],
        out_shape=[jax.ShapeDtypeStruct((bsz, seq, D_MODEL), BF16),
                   jax.ShapeDtypeStruct((bsz, A_HEADS, A_DK, A_DV), F32),
                   jax.ShapeDtypeStruct((bsz, B_HEADS, B_DK, B_DV), F32),
                   jax.ShapeDtypeStruct((bsz, C_HEADS, C_DK, C_DV), F32)],
        scratch_shapes=[pltpu.VMEM((n_par, A_W, A_W), F32), pltpu.VMEM((n_par, B_W, B_W), F32),
                        pltpu.VMEM((n_par, C_QP, C_VW), F32)],
        compiler_params=pltpu.CompilerParams(dimension_semantics=("parallel", "arbitrary"),
                                             vmem_limit_bytes=VMEM_LIMIT),
        name="mix_prompt",
    )(proj, cos, sin, lbl, hn, w2, gb, gn)


def _head_recurrence(q, k, v, logd, gate, norm_col, s_ref, snew_ref, y_ref, qb_ref, kb_ref, eb_ref):
    n_t = len(q)
    dk, n_seq = q[0].shape
    dv = v[0].shape[0]
    b = [logd[0]]
    for t in range(1, n_t):
        b.append(b[-1] + logd[t])
    o = []
    for t in range(n_t):
        acc = jnp.sum(q[t] * k[t], axis=0, keepdims=True) * v[t]
        for s in range(t):
            sc = jnp.sum(q[t] * k[s] * jnp.exp(b[t] - b[s]), axis=0, keepdims=True)
            acc = acc + sc * v[s]
        o.append(acc)
    rep = lambda x: jnp.broadcast_to(x[:, None, :], (dk, SUBLANES, n_seq))
    for t in range(n_t):
        qb_ref[t] = rep(q[t] * jnp.exp(b[t]))
        kb_ref[t] = rep(k[t] * jnp.exp(b[n_t - 1] - b[t]))
    eb_ref[...] = rep(jnp.exp(b[n_t - 1]))
    chunks = [[] for _ in range(n_t)]
    for vc in range(dv // SUBLANES):
        rows = pl.ds(vc * SUBLANES, SUBLANES)
        v_rows = [v[t][vc * SUBLANES:(vc + 1) * SUBLANES, :] for t in range(n_t)]

        def body(d, carry):
            s_d = s_ref[d, rows, :]
            new = eb_ref[d] * s_d
            out = []
            for t in range(n_t):
                new = new + kb_ref[t, d] * v_rows[t]
                out.append(carry[t] + qb_ref[t, d] * s_d)
            snew_ref[d, rows, :] = new
            return tuple(out)

        inter = lax.fori_loop(0, dk, body, tuple(jnp.zeros((SUBLANES, n_seq), F32) for _ in range(n_t)))
        for t in range(n_t):
            chunks[t].append(inter[t])
    for t in range(n_t):
        o_t = o[t] + jnp.concatenate(chunks[t], axis=0)
        ms = jnp.mean(o_t * o_t, axis=0, keepdims=True)
        y_t = o_t * lax.rsqrt(ms + EPS) * _silu(gate[t])
        if norm_col is not None:
            y_t = y_t * norm_col
        y_ref[:, t * n_seq:(t + 1) * n_seq] = y_t.astype(BF16)


def _slabs(x, n_t):
    n_seq = x.shape[1] // n_t
    return [x[:, t * n_seq:(t + 1) * n_seq] for t in range(n_t)]


def _head_rows(ref, dh):
    h = pl.program_id(0)
    return ref[pl.ds(pl.multiple_of(h * dh, 2 * SUBLANES), dh), :]


def _hgrn_head_kernel(q_ref, f_ref, i_ref, g_ref, lbl_ref, hn_ref, s_ref, *rest, layer, n_t):
    snew_ref, y_ref, qb_ref, kb_ref, eb_ref = rest[-5:]
    log_f, k_a = _hgrn_gate(f_ref[...], [lbl_ref[i] for i in range(DEPTH)], layer)
    _head_recurrence(_slabs(_silu(q_ref[...]), n_t), _slabs(k_a, n_t), _slabs(i_ref[...], n_t),
                     _slabs(log_f, n_t), _slabs(g_ref[...], n_t), hn_ref[...], s_ref, snew_ref, y_ref,
                     qb_ref, kb_ref, eb_ref)


def _ret_head_kernel(q_ref, k_ref, v_ref, g_ref, cos_ref, sin_ref, lg_ref, s_ref, *rest, n_t):
    snew_ref, y_ref, qb_ref, kb_ref, eb_ref = rest[-5:]
    half = B_DK // 2

    def rot(x):
        x1, x2 = x[:half], x[half:]
        c, s = cos_ref[...], sin_ref[...]
        return jnp.concatenate([x1 * c - x2 * s, x1 * s + x2 * c], axis=0)

    q = rot(q_ref[...])
    k = rot(k_ref[...]) * (B_DK ** -0.5)
    logd = jnp.broadcast_to(lg_ref[...], (B_DK, q.shape[1] // n_t))
    _head_recurrence(_slabs(q, n_t), _slabs(k, n_t), _slabs(v_ref[...], n_t), [logd] * n_t,
                     _slabs(g_ref[...], n_t), None, s_ref, snew_ref, y_ref, qb_ref, kb_ref, eb_ref)


def _gla_head_kernel(q_ref, k_ref, v_ref, g_ref, r_ref, w2t_ref, gb_ref, gn_ref, s_ref, *rest, n_t):
    snew_ref, y_ref, qb_ref, kb_ref, eb_ref = rest[-5:]
    gk = _dot(_head_rows(w2t_ref, C_DK), r_ref[...].astype(BF16)) + _head_rows(gb_ref, C_DK)
    log_a = jax.nn.log_sigmoid(gk) * (1.0 / C_TAU)
    _head_recurrence(_slabs(_head_rows(q_ref, C_DK) * (C_DK ** -0.5), n_t), _slabs(_head_rows(k_ref, C_DK), n_t),
                     _slabs(_head_rows(v_ref, C_DV), n_t), _slabs(log_a, n_t),
                     _slabs(_head_rows(g_ref, C_DV), n_t), gn_ref[...], s_ref, snew_ref, y_ref,
                     qb_ref, kb_ref, eb_ref)


def _head_call(body, name, projt, segs, extras, extra_specs, state, prev, layer, n_heads, dk, dv, n_t):
    m = projt.shape[1]
    n_seq = m // n_t
    seg_specs = []
    for off, rows, per_head in segs:
        assert off % rows == 0
        if per_head:
            seg_specs.append(pl.BlockSpec((rows, m), functools.partial(lambda i, h: (i + h, 0), off // rows)))
        else:
            seg_specs.append(pl.BlockSpec((rows, m), functools.partial(lambda i, h: (i, 0), off // rows)))
    st_spec = pl.BlockSpec((None, None, dk, dv, n_seq), lambda h: (layer, h, 0, 0, 0))
    in_specs = seg_specs + extra_specs + [st_spec]
    operands = [projt] * len(segs) + extras + [state]
    aliases = {}
    if prev is not None:
        in_specs.append(pl.BlockSpec(memory_space=pl.ANY))
        operands.append(prev)
        aliases = {len(operands) - 1: 0}
    rep_rows = pltpu.VMEM((n_t, dk, SUBLANES, n_seq), F32)
    return pl.pallas_call(
        body,
        grid=(n_heads,),
        in_specs=in_specs,
        out_specs=[st_spec, pl.BlockSpec((dv, m), lambda h: (h, 0))],
        out_shape=[jax.ShapeDtypeStruct(state.shape, F32), jax.ShapeDtypeStruct((n_heads * dv, m), BF16)],
        scratch_shapes=[rep_rows, rep_rows, pltpu.VMEM((dk, SUBLANES, n_seq), F32)],
        input_output_aliases=aliases,
        compiler_params=pltpu.CompilerParams(dimension_semantics=("parallel",), vmem_limit_bytes=VMEM_LIMIT),
        name=name,
    )(*operands)


def _rope_angles(pos):
    half = B_DK // 2
    inv_freq = ROPE_BASE ** (-jnp.arange(half, dtype=F32) / half)
    return pos[:, None] * inv_freq[None, :]


def _rope_tables(pos):
    ang = _rope_angles(pos)
    cos, sin = jnp.cos(ang), jnp.sin(ang)
    cos_h = jnp.concatenate([cos, cos], axis=1)
    sin_h = jnp.concatenate([-sin, sin], axis=1)
    return jnp.tile(cos_h, (1, B_HEADS)), jnp.tile(sin_h, (1, B_HEADS))


def _rope_tables_channel_major(pos, n_seq):
    ang = _rope_angles(pos)
    rep = lambda a: jnp.repeat(a.T, n_seq, axis=1)
    return rep(jnp.cos(ang)), rep(jnp.sin(ang))


def _pad_w_in(w):
    z = lambda n: jnp.zeros(w.shape[:2] + (n,), w.dtype)
    c0 = 2560
    return jnp.concatenate([
        w[..., :c0],
        w[..., c0:c0 + C_QW], z(C_QP - C_QW),
        w[..., c0 + C_QW:c0 + 2 * C_QW], z(C_QP - C_QW),
        w[..., c0 + 2 * C_QW:c0 + 2 * C_QW + 2 * C_VW],
        w[..., c0 + 2 * C_QW + 2 * C_VW:], z(C_RP - C_RANK)], axis=-1)


def kernel(x_prompt, x_sample, state_hgrn, state_ret, state_gla, ffn1_norm, ffn1_w_in, ffn1_w_out, mix_norm, w_in, hgrn_lb_logits, hgrn_norm, gla_w2, gla_b, gla_norm, w_out, ffn2_norm, ffn2_w_in, ffn2_w_out, final_norm):
    bsz, seq, _ = x_prompt.shape
    dbs, dseq, _ = x_sample.shape
    assert seq % ROWS == 0 and (bsz * seq) % TM == 0 and (dbs * dseq) % TM == 0 and dbs % LANES == 0

    f1_in, f1_out = ffn1_w_in.astype(BF16), ffn1_w_out.astype(BF16)
    f2_in, f2_out = ffn2_w_in.astype(BF16), ffn2_w_out.astype(BF16)
    wo = w_out.astype(BF16)
    wmix = _pad_w_in(w_in.astype(BF16))
    stacked = lambda a: a.reshape(DEPTH, 1, -1).astype(F32)
    n1, nm, n2 = stacked(ffn1_norm), stacked(mix_norm), stacked(ffn2_norm)
    nf = final_norm.reshape(1, -1).astype(F32)
    lbl = hgrn_lb_logits.astype(F32)
    w2_pad = jnp.zeros((DEPTH, C_RP, C_QP), BF16).at[:, :C_RANK, :C_QW].set(gla_w2.astype(BF16))
    gb_pad = jnp.zeros((DEPTH, 1, C_QP), F32).at[:, 0, :C_QW].set(gla_b)

    cos_p, sin_p = _rope_tables(jnp.arange(seq, dtype=F32))
    x = x_prompt.reshape(bsz * seq, D_MODEL)
    p_states = []
    for l in range(DEPTH):
        x1, proj = _ffn_proj(x, n1, f1_in, f1_out, nm, wmix, l, False)
        y, sa, sb, sc = _mix_prompt(proj.reshape(bsz, seq, NP), cos_p, sin_p, lbl,
                                    jnp.tile(hgrn_norm[l], A_HEADS).reshape(1, A_W), w2_pad[l], gb_pad[l],
                                    jnp.tile(gla_norm[l], C_HEADS).reshape(1, C_VW), l)
        x = _out_ffn(x1, [y.reshape(bsz * seq, D_MODEL)], wo, n2, f2_in, f2_out, nf, l, False)
        p_states.append((sa, sb, sc))
    y_prompt = x.reshape(bsz, seq, D_MODEL)

    m = dbs * dseq
    x = jnp.transpose(x_sample, (1, 0, 2)).reshape(m, D_MODEL)
    states = [jnp.transpose(s, (0, 2, 3, 4, 1)) for s in (state_hgrn, state_ret, state_gla)]
    new = [None, None, None]
    wmix_t = jnp.swapaxes(wmix, 1, 2)
    cos_s, sin_s = _rope_tables_channel_major(PAST_LEN + jnp.arange(dseq, dtype=F32), dbs)
    lg = jnp.asarray(_LOG_GAMMA, F32).reshape(B_HEADS, 1, 1)
    lbl_col = lbl.reshape(DEPTH, A_W, 1)
    w2_t = jnp.swapaxes(w2_pad, 1, 2)
    gb_col = jnp.swapaxes(gb_pad, 1, 2)
    whole = lambda shape: pl.BlockSpec(shape, lambda h: (0,) * len(shape))
    per_head = lambda off, rows: (off, rows, True)
    segment = lambda off, rows: (off, rows, False)
    for l in range(DEPTH):
        x1, projt = _ffn_proj(x, n1, f1_in, f1_out, nm, wmix_t, l, True)
        new[0], ya = _head_call(
            functools.partial(_hgrn_head_kernel, layer=l, n_t=dseq), "mix_sample_hgrn", projt,
            [per_head(O_AQ, A_DK), per_head(O_AF, A_DK), per_head(O_AI, A_DV), per_head(O_AG, A_DV)],
            [lbl_col, hgrn_norm[l].reshape(A_DV, 1)],
            [pl.BlockSpec((DEPTH, A_DK, 1), lambda h: (0, h, 0)), whole((A_DV, 1))],
            states[0], new[0], l, A_HEADS, A_DK, A_DV, dseq)
        new[1], yb = _head_call(
            functools.partial(_ret_head_kernel, n_t=dseq), "mix_sample_ret", projt,
            [per_head(O_BQ, B_DK), per_head(O_BK, B_DK), per_head(O_BV, B_DV), per_head(O_BG, B_DV)],
            [cos_s, sin_s, lg],
            [whole((B_DK // 2, m)), whole((B_DK // 2, m)), pl.BlockSpec((None, 1, 1), lambda h: (h, 0, 0))],
            states[1], new[1], l, B_HEADS, B_DK, B_DV, dseq)
        new[2], yc = _head_call(
            functools.partial(_gla_head_kernel, n_t=dseq), "mix_sample_gla", projt,
            [segment(O_CQ, C_QP), segment(O_CK, C_QP), segment(O_CV, C_VW), segment(O_CG, C_VW),
             segment(O_CR, C_RP)],
            [w2_t[l], gb_col[l], gla_norm[l].reshape(C_DV, 1)],
            [whole((C_QP, C_RP)), whole((C_QP, 1)), whole((C_DV, 1))],
            states[2], new[2], l, C_HEADS, C_DK, C_DV, dseq)
        x = _out_ffn(x1, [ya, yb, yc], wo, n2, f2_in, f2_out, nf, l, True)
    y_sample = jnp.transpose(x.reshape(dseq, dbs, D_MODEL), (1, 0, 2))
    s_states = [jnp.transpose(s, (0, 4, 1, 2, 3)) for s in new]

    stack = lambda i: jnp.stack([s[i] for s in p_states])
    return (y_prompt, y_sample, stack(0), stack(1), stack(2), s_states[0], s_states[1], s_states[2])
```

```python
import functools
import math

import jax
import jax.numpy as jnp
from jax import lax
from jax.experimental import pallas as pl
from jax.experimental.pallas import tpu as pltpu

F32, BF16 = jnp.float32, jnp.bfloat16

D_MODEL = 1024
DEPTH = 2
A_HEADS, A_DK, A_DV = 4, 64, 64
B_HEADS, B_DK, B_DV = 6, 64, 64
C_HEADS, C_DK, C_DV = 4, 48, 96
C_RANK = 16
C_TAU = 16.0
D_FF = 2816
ROPE_BASE = 10000.0
EPS = 1e-6
PAST_LEN = 16384

A_W = A_HEADS * A_DK
B_W = B_HEADS * B_DK
C_QW = C_HEADS * C_DK
C_VW = C_HEADS * C_DV
LANES = 128
SUBLANES = 8
MXU_DIM = 256
C_QP = 256
C_RP = LANES

O_AQ, O_AF, O_AI, O_AG = 0, 256, 512, 768
O_BQ, O_BK, O_BV, O_BG = 1024, 1408, 1792, 2176
O_CQ, O_CK, O_CV, O_CG, O_CR = 2560, 2816, 3072, 3456, 3840
NP = O_CR + C_RP

ROWS = 128
SEQS_PER_STEP = 4
FACTOR_BLOCK = 64
FACTOR_MAX_EXP = 80.0
STATE_ROW_UNROLL = 4
FF_CHUNK = 256
TM_IN = 256
TM_OUT = 512
VMEM_LIMIT = 56 * 1024 * 1024


def _dot(a, b):
    return jnp.dot(a, b, preferred_element_type=F32)


def _dot_nt(a, b):
    return lax.dot_general(a, b, (((1,), (1,)), ((), ())), preferred_element_type=F32)


def _dot_tn(a, b):
    return lax.dot_general(a, b, (((0,), (0,)), ((), ())), preferred_element_type=F32)


def _iota(shape, dim):
    return lax.broadcasted_iota(jnp.int32, shape, dim)


def _split3(x):
    hi = x.astype(BF16)
    r = x - hi.astype(F32)
    mid = r.astype(BF16)
    lo = (r - mid.astype(F32)).astype(BF16)
    return hi, mid, lo


def _dot01(mat01, x):
    hi, mid, lo = _split3(x)
    return _dot(mat01, hi) + _dot(mat01, mid) + _dot(mat01, lo)


def _rms(x, w):
    ms = jnp.mean(x * x, axis=-1, keepdims=True)
    return x * lax.rsqrt(ms + EPS) * w


def _silu(x):
    return x * jax.nn.sigmoid(x)


def _swiglu(h_bf, win_ref, wout_ref, act_ref):
    for c in range(D_FF // FF_CHUNK):
        lo = c * FF_CHUNK
        g = _dot(h_bf, win_ref[:, lo:lo + FF_CHUNK])
        u = _dot(h_bf, win_ref[:, D_FF + lo:D_FF + lo + FF_CHUNK])
        act_ref[:, lo:lo + FF_CHUNK] = (_silu(g) * u).astype(BF16)
    return _dot(act_ref[...], wout_ref[...])


def _lower_bound(logit_rows, layer):
    mx = functools.reduce(jnp.maximum, logit_rows)
    ex = [jnp.exp(r - mx) for r in logit_rows]
    tot = functools.reduce(lambda a, c: a + c, ex)
    acc = functools.reduce(lambda a, c: a + c, ex[:layer + 1]) / tot
    return acc - ex[0] / tot


def _hgrn_gate(af, logit_rows, layer):
    if layer == 0:
        return jax.nn.log_sigmoid(af), jax.nn.sigmoid(-af)
    lb = _lower_bound(logit_rows, layer)
    log_f = jnp.logaddexp(jnp.log(lb), jnp.log1p(-lb) + jax.nn.log_sigmoid(af))
    return log_f, (1.0 - lb) * jax.nn.sigmoid(-af)


def _ffn_proj_kernel(x_ref, n1_ref, win_ref, wout_ref, nm_ref, wmix_ref, x1_ref, proj_ref, act_ref, *,
                     channel_major):
    x = x_ref[...]
    h = _rms(x, n1_ref[...]).astype(BF16)
    x1 = x + 0.5 * _swiglu(h, win_ref, wout_ref, act_ref)
    x1_ref[...] = x1
    hm = _rms(x1, nm_ref[...]).astype(BF16)
    proj_ref[...] = _dot_nt(wmix_ref[...], hm) if channel_major else _dot(hm, wmix_ref[...])


def _out_ffn_kernel(x_ref, *refs, final, channel_major):
    n_y = 3 if channel_major else 1
    y_refs = refs[:n_y]
    wo_ref, n2_ref, win_ref, wout_ref, nf_ref, o_ref, act_ref = refs[n_y:]
    x2 = x_ref[...]
    if channel_major:
        lo = 0
        for y_ref in y_refs:
            w = y_ref.shape[0]
            x2 = x2 + _dot_tn(y_ref[...], wo_ref[lo:lo + w, :])
            lo += w
    else:
        x2 = x2 + _dot(y_refs[0][...], wo_ref[...])
    h = _rms(x2, n2_ref[...]).astype(BF16)
    x3 = x2 + 0.5 * _swiglu(h, win_ref, wout_ref, act_ref)
    if final:
        x3 = _rms(x3, nf_ref[...])
    o_ref[...] = x3


def _resident(shape, layer=None):
    if layer is None:
        return pl.BlockSpec(shape, lambda i: (0,) * len(shape), pipeline_mode=pl.Buffered(1))
    return pl.BlockSpec((None,) + shape, lambda i: (layer,) + (0,) * len(shape), pipeline_mode=pl.Buffered(1))


def _row_tile(tm, width):
    return pl.BlockSpec((tm, width), lambda i: (i, 0))


def _col_tile(tm, height):
    return pl.BlockSpec((height, tm), lambda i: (0, i))


def _ffn_proj(x, n1, win, wout, nm, wmix, layer, channel_major):
    m = x.shape[0]
    tm = TM_IN
    wmix_shape = (NP, D_MODEL) if channel_major else (D_MODEL, NP)
    return pl.pallas_call(
        functools.partial(_ffn_proj_kernel, channel_major=channel_major),
        grid=(m // tm,),
        in_specs=[_row_tile(tm, D_MODEL), _resident((1, D_MODEL), layer), _resident((D_MODEL, 2 * D_FF), layer),
                  _resident((D_FF, D_MODEL), layer), _resident((1, D_MODEL), layer), _resident(wmix_shape, layer)],
        out_specs=[_row_tile(tm, D_MODEL), _col_tile(tm, NP) if channel_major else _row_tile(tm, NP)],
        out_shape=[jax.ShapeDtypeStruct((m, D_MODEL), F32),
                   jax.ShapeDtypeStruct((NP, m) if channel_major else (m, NP), F32)],
        scratch_shapes=[pltpu.VMEM((tm, D_FF), BF16)],
        compiler_params=pltpu.CompilerParams(dimension_semantics=("parallel",), vmem_limit_bytes=VMEM_LIMIT),
        name="ffn1_proj",
    )(x, n1, win, wout, nm, wmix)


def _out_ffn(x, ys, wo, n2, win, wout, nf, layer, channel_major):
    m = x.shape[0]
    tm = TM_OUT
    y_specs = [_col_tile(tm, y.shape[0]) for y in ys] if channel_major else [_row_tile(tm, D_MODEL)]
    return pl.pallas_call(
        functools.partial(_out_ffn_kernel, final=layer == DEPTH - 1, channel_major=channel_major),
        grid=(m // tm,),
        in_specs=[_row_tile(tm, D_MODEL)] + y_specs + [
            _resident((D_MODEL, D_MODEL), layer), _resident((1, D_MODEL), layer),
            _resident((D_MODEL, 2 * D_FF), layer), _resident((D_FF, D_MODEL), layer), _resident((1, D_MODEL))],
        out_specs=_row_tile(tm, D_MODEL),
        out_shape=jax.ShapeDtypeStruct((m, D_MODEL), F32),
        scratch_shapes=[pltpu.VMEM((tm, D_FF), BF16)],
        compiler_params=pltpu.CompilerParams(dimension_semantics=("parallel",), vmem_limit_bytes=VMEM_LIMIT),
        name="outproj_ffn2",
    )(x, *ys, wo, n2, win, wout, nf)


def _log2(n):
    assert n & (n - 1) == 0
    return n.bit_length() - 1


def _head_mask(width, dh, h):
    lane = _iota((1, width), 1)
    return (lane >= dh * h) & (lane < dh * (h + 1))


def _head_grid_mask(rows, cols, dr, dc, n_heads):
    r = _iota((rows, cols), 0)
    c = _iota((rows, cols), 1)
    m = None
    for h in range(n_heads):
        mh = (r >= dr * h) & (r < dr * (h + 1)) & (c >= dc * h) & (c < dc * (h + 1))
        m = mh if m is None else (m | mh)
    return m


def _seg_mean_sq(o, dh, n_heads):
    w = o.shape[1]
    if w > MXU_DIM and MXU_DIM % dh == 0:
        parts = [_seg_mean_sq(o[:, lo:min(lo + MXU_DIM, w)], dh, (min(lo + MXU_DIM, w) - lo) // dh)
                 for lo in range(0, w, MXU_DIM)]
        return jnp.concatenate(parts, axis=1)
    ones = jnp.where(_head_grid_mask(w, w, dh, dh, n_heads), 1.0, 0.0).astype(BF16)
    sq = o * o
    hi = sq.astype(BF16)
    lo = (sq - hi.astype(F32)).astype(BF16)
    return (_dot(hi, ones) + _dot(lo, ones)) * (1.0 / dh)


def _head_norm(o, dh, n_heads):
    return o * lax.rsqrt(_seg_mean_sq(o, dh, n_heads) + EPS)


def _tile_cumsum(g):
    rows = _iota((ROWS, ROWS), 0)
    cols = _iota((ROWS, ROWS), 1)
    return _dot01(jnp.where(cols <= rows, 1.0, 0.0).astype(BF16), g)


def _block_row(b, size, idx):
    c = b.shape[1]
    if size >= SUBLANES:
        b3 = b.reshape(ROWS // size, size, c)
        return jnp.broadcast_to(b3[:, idx:idx + 1, :], b3.shape).reshape(ROWS, c)
    b8 = b.reshape(ROWS // SUBLANES, SUBLANES, c)
    sub = _iota(b8.shape, 1)
    ref = b8[:, idx:idx + 1, :]
    for blk in range(1, SUBLANES // size):
        lo = blk * size
        ref = jnp.where(sub >= lo, b8[:, lo + idx:lo + idx + 1, :], ref)
    return jnp.broadcast_to(ref, b8.shape).reshape(ROWS, c)


def _stack_heads(x, dh, n_heads, extra_mask=None):
    parts = []
    for h in range(n_heads):
        m = _head_mask(x.shape[1], dh, h)
        if extra_mask is not None:
            m = m & extra_mask
        parts.append(jnp.where(m, x, 0.0))
    return jnp.concatenate(parts, axis=0).astype(BF16)


def _apply_scores(p, v, dv, n_heads):
    pcat = jnp.concatenate([p[h].astype(BF16) for h in range(n_heads)], axis=1)
    return _dot(pcat, _stack_heads(v, dv, n_heads))


def _tree_levels(q, k, b, dk, n_heads, first):
    rows = _iota((ROWS, ROWS), 0)
    cols = _iota((ROWS, ROWS), 1)
    rowc = _iota((ROWS, q.shape[1]), 0)
    total = jnp.zeros((n_heads, ROWS, ROWS), F32)
    m = first
    while m < ROWS:
        ref = _block_row(b, 2 * m, m - 1)
        up = (rowc & (2 * m - 1)) >= m
        e = jnp.exp(jnp.where(up, b - ref, ref - b))
        ql = _stack_heads(q * e, dk, n_heads, extra_mask=up)
        kl = jnp.where(up, 0.0, k * e).astype(BF16)
        sc = _dot_nt(ql, kl).reshape(n_heads, ROWS, ROWS)
        sh = _log2(2 * m)
        total = total + jnp.where(((rows >> sh) == (cols >> sh))[None], sc, 0.0)
        m *= 2
    return total


def _block_deviation(b):
    return b - _block_row(b, FACTOR_BLOCK, FACTOR_BLOCK // 2 - 1)


def _gated_intra(q, k, v, b, dev, dk, dv, n_heads, factorise):
    rows = _iota((ROWS, ROWS), 0)
    cols = _iota((ROWS, ROWS), 1)
    if factorise:
        qf = _stack_heads(q * jnp.exp(dev), dk, n_heads)
        kf = (k * jnp.exp(-dev)).astype(BF16)
        sc = _dot_nt(qf, kf).reshape(n_heads, ROWS, ROWS)
        sh = _log2(FACTOR_BLOCK)
        keep = ((rows >> sh) == (cols >> sh)) & (cols <= rows)
        total = jnp.where(keep[None], sc, 0.0) + _tree_levels(q, k, b, dk, n_heads, FACTOR_BLOCK)
    else:
        sc = _dot_nt(_stack_heads(q, dk, n_heads), k.astype(BF16)).reshape(n_heads, ROWS, ROWS)
        total = jnp.where((rows == cols)[None], sc, 0.0) + _tree_levels(q, k, b, dk, n_heads, 1)
    return _apply_scores(total, v, dv, n_heads)


def _column(row_vec):
    c = row_vec.shape[1]
    return jnp.transpose(jnp.broadcast_to(row_vec, (SUBLANES, c)))[:, 0:1]


def _gated_tile(q, k, v, b, dev, dk, dv, n_heads, factorise, s_ref):
    o = _gated_intra(q, k, v, b, dev, dk, dv, n_heads, factorise)
    s0 = s_ref[...]
    o = o + _dot((q * jnp.exp(b)).astype(BF16), s0.astype(BF16))
    b_last = b[ROWS - 1:ROWS, :]
    khat = (k * jnp.exp(b_last - b)).astype(BF16)
    ds = _dot_tn(khat, v.astype(BF16))
    keep = _head_grid_mask(q.shape[1], v.shape[1], dk, dv, n_heads)
    s_ref[...] = s0 * _column(jnp.exp(b_last)) + jnp.where(keep, ds, 0.0)
    return o


_LOG_GAMMA = tuple(math.log(1.0 - 2.0 ** (-5.0 - h)) for h in range(B_HEADS))


def _per_head_lanes(width, dh, values):
    lane = _iota((1, width), 1)
    out = jnp.zeros((1, width), F32)
    for h, val in enumerate(values):
        out = jnp.where((lane >= dh * h) & (lane < dh * (h + 1)), val, out)
    return out


def _rotary(x, cos, sin_signed):
    w = x.shape[1]
    lane = _iota((1, w), 1)
    first_half = (lane & (B_DK - 1)) < (B_DK // 2)
    partner = jnp.where(first_half, pltpu.roll(x, w - B_DK // 2, 1), pltpu.roll(x, B_DK // 2, 1))
    return x * cos + partner * sin_signed


def _lane_groups(width):
    return [(lo, min(lo + MXU_DIM, width)) for lo in range(0, width, MXU_DIM)]


def _retention_tile(q, k, v, s_ref):
    rows = _iota((ROWS, ROWS), 0)
    cols = _iota((ROWS, ROWS), 1)
    causal = cols <= rows
    dist = (rows - cols).astype(F32)
    tau = _iota((ROWS, 1), 0).astype(F32)
    lg = _per_head_lanes(B_W, B_DK, _LOG_GAMMA)
    qhat = (q * jnp.exp(lg * (tau + 1.0))).astype(BF16)
    khat = (k * jnp.exp(lg * (ROWS - 1.0 - tau))).astype(BF16)
    decay = _column(_per_head_lanes(B_W, B_DK, [math.exp(ROWS * g) for g in _LOG_GAMMA]))
    v_bf = v.astype(BF16)
    outs = []
    for lo, hi in _lane_groups(B_W):
        nh = (hi - lo) // B_DK
        sc = _dot_nt(_stack_heads(q[:, lo:hi], B_DK, nh), k[:, lo:hi].astype(BF16)).reshape(nh, ROWS, ROWS)
        gam = jnp.stack([jnp.where(causal, jnp.exp(dist * g), 0.0) for g in _LOG_GAMMA[lo // B_DK:hi // B_DK]])
        s0 = s_ref[lo:hi, lo:hi]
        outs.append(_apply_scores(sc * gam, v[:, lo:hi], B_DV, nh) + _dot(qhat[:, lo:hi], s0.astype(BF16)))
        ds = _dot_tn(khat[:, lo:hi], v_bf[:, lo:hi])
        keep = _head_grid_mask(hi - lo, hi - lo, B_DK, B_DV, nh)
        s_ref[lo:hi, lo:hi] = s0 * decay[lo:hi, :] + jnp.where(keep, ds, 0.0)
    return jnp.concatenate(outs, axis=1)


def _mix_gates(proj_ref, lbl_ref, w2_ref, gb_ref, layer):
    seg = lambda off, w: proj_ref[:, off:off + w]
    log_f, k_a = _hgrn_gate(seg(O_AF, A_W), [lbl_ref[i:i + 1, :] for i in range(DEPTH)], layer)
    gk = _dot(seg(O_CR, C_RP).astype(BF16), w2_ref[...]) + gb_ref[...]
    log_a = jax.nn.log_sigmoid(gk) * (1.0 / C_TAU)
    b_a = _tile_cumsum(log_f)
    b_c = _tile_cumsum(log_a)
    dev_a = _block_deviation(b_a)
    dev_c = _block_deviation(b_c)
    worst = jnp.maximum(jnp.max(jnp.abs(dev_a)), jnp.max(jnp.abs(dev_c)))
    return (k_a, b_a, dev_a, b_c, dev_c), worst


def _mix_rest(proj_ref, gates, cos_ref, sin_ref, hn_ref, gn_ref, sa_ref, sb_ref, sc_ref, factorise):
    seg = lambda off, w: proj_ref[:, off:off + w]
    k_a, b_a, dev_a, b_c, dev_c = gates
    o_a = _gated_tile(_silu(seg(O_AQ, A_W)), k_a, seg(O_AI, A_W), b_a, dev_a, A_DK, A_DV, A_HEADS, factorise,
                      sa_ref)
    y_a = _head_norm(o_a, A_DV, A_HEADS) * hn_ref[...] * _silu(seg(O_AG, A_W))
    q_b = _rotary(seg(O_BQ, B_W), cos_ref[...], sin_ref[...])
    k_b = _rotary(seg(O_BK, B_W), cos_ref[...], sin_ref[...]) * (B_DK ** -0.5)
    o_b = _retention_tile(q_b, k_b, seg(O_BV, B_W), sb_ref)
    y_b = _head_norm(o_b, B_DV, B_HEADS) * _silu(seg(O_BG, B_W))
    o_c = _gated_tile(seg(O_CQ, C_QP) * (C_DK ** -0.5), seg(O_CK, C_QP), seg(O_CV, C_VW), b_c, dev_c,
                      C_DK, C_DV, C_HEADS, factorise, sc_ref)
    y_c = _head_norm(o_c, C_DV, C_HEADS) * gn_ref[...] * _silu(seg(O_CG, C_VW))
    return jnp.concatenate([y_a, y_b, y_c], axis=1).astype(BF16)


def _mix_prompt_kernel(proj_ref, cos_ref, sin_ref, lbl_ref, hn_ref, w2_ref, gb_ref, gn_ref,
                       y_ref, sa_ref, sb_ref, sc_ref, sa_acc, sb_acc, sc_acc, *, layer):
    t = pl.program_id(1)

    @pl.when(t == 0)
    def _():
        sa_acc[...] = jnp.zeros_like(sa_acc)
        sb_acc[...] = jnp.zeros_like(sb_acc)
        sc_acc[...] = jnp.zeros_like(sc_acc)

    n_par = proj_ref.shape[0]
    gates, worst = [], None
    for i in range(n_par):
        g, w = _mix_gates(proj_ref.at[i], lbl_ref, w2_ref, gb_ref, layer)
        gates.append(g)
        worst = w if worst is None else jnp.maximum(worst, w)

    def body(factorise):
        for i in range(n_par):
            y_ref[i] = _mix_rest(proj_ref.at[i], gates[i], cos_ref, sin_ref, hn_ref, gn_ref,
                                 sa_acc.at[i], sb_acc.at[i], sc_acc.at[i], factorise)

    safe = worst < FACTOR_MAX_EXP

    @pl.when(safe)
    def _():
        body(True)

    @pl.when(jnp.logical_not(safe))
    def _():
        body(False)

    @pl.when(t == pl.num_programs(1) - 1)
    def _():
        for i in range(n_par):
            for h in range(A_HEADS):
                sa_ref[i, h] = sa_acc[i, A_DK * h:A_DK * (h + 1), A_DV * h:A_DV * (h + 1)]
            for h in range(B_HEADS):
                sb_ref[i, h] = sb_acc[i, B_DK * h:B_DK * (h + 1), B_DV * h:B_DV * (h + 1)]
            for h in range(C_HEADS):
                sc_ref[i, h] = sc_acc[i, C_DK * h:C_DK * (h + 1), C_DV * h:C_DV * (h + 1)]


def _const(shape):
    return pl.BlockSpec(shape, lambda *_: (0,) * len(shape))


def _mix_prompt(proj, cos, sin, lbl, hn, w2, gb, gn, layer):
    bsz, seq, _ = proj.shape
    n_par = SEQS_PER_STEP if bsz % SEQS_PER_STEP == 0 else 1
    tile = lambda w: pl.BlockSpec((n_par, ROWS, w), lambda b, t: (b, t, 0))
    tab = pl.BlockSpec((ROWS, B_W), lambda b, t: (t, 0))
    st = lambda h, dk, dv: pl.BlockSpec((n_par, h, dk, dv), lambda b, t: (b, 0, 0, 0))
    return pl.pallas_call(
        functools.partial(_mix_prompt_kernel, layer=layer),
        grid=(bsz // n_par, seq // ROWS),
        in_specs=[tile(NP), tab, tab, _const((DEPTH, A_W)), _const((1, A_W)), _const((C_RP, C_QP)),
                  _const((1, C_QP)), _const((1, C_VW))],
        out_specs=[tile(D_MODEL), st(A_HEADS, A_DK, A_DV), st(B_HEADS, B_DK, B_DV), st(C_HEADS, C_DK, C_DV)],
        out_shape=[jax.ShapeDtypeStruct((bsz, seq, D_MODEL), BF16),
                   jax.ShapeDtypeStruct((bsz, A_HEADS, A_DK, A_DV), F32),
                   jax.ShapeDtypeStruct((bsz, B_HEADS, B_DK, B_DV), F32),
                   jax.ShapeDtypeStruct((bsz, C_HEADS, C_DK, C_DV), F32)],
        scratch_shapes=[pltpu.VMEM((n_par, A_W, A_W), F32), pltpu.VMEM((n_par, B_W, B_W), F32),
                        pltpu.VMEM((n_par, C_QP, C_VW), F32)],
        compiler_params=pltpu.CompilerParams(dimension_semantics=("parallel", "arbitrary"),
                                             vmem_limit_bytes=VMEM_LIMIT),
        name="mix_prompt",
    )(proj, cos, sin, lbl, hn, w2, gb, gn)


def _head_recurrence(q, k, v, logd, gate, norm_col, s_ref, snew_ref, y_ref, qb_ref, kb_ref, eb_ref):
    n_t = len(q)
    dk, n_seq = q[0].shape
    dv = v[0].shape[0]
    b = [logd[0]]
    for t in range(1, n_t):
        b.append(b[-1] + logd[t])
    o = []
    for t in range(n_t):
        acc = jnp.sum(q[t] * k[t], axis=0, keepdims=True) * v[t]
        for s in range(t):
            sc = jnp.sum(q[t] * k[s] * jnp.exp(b[t] - b[s]), axis=0, keepdims=True)
            acc = acc + sc * v[s]
        o.append(acc)
    rep = lambda x: jnp.broadcast_to(x[:, None, :], (dk, SUBLANES, n_seq))
    for t in range(n_t):
        qb_ref[t] = rep(q[t] * jnp.exp(b[t]))
        kb_ref[t] = rep(k[t] * jnp.exp(b[n_t - 1] - b[t]))
    eb_ref[...] = rep(jnp.exp(b[n_t - 1]))
    chunks = [[] for _ in range(n_t)]
    for vc in range(dv // SUBLANES):
        rows = pl.ds(vc * SUBLANES, SUBLANES)
        v_rows = [v[t][vc * SUBLANES:(vc + 1) * SUBLANES, :] for t in range(n_t)]

        def body(d, carry):
            s_d = s_ref[d, rows, :]
            terms = [eb_ref[d] * s_d] + [kb_ref[t, d] * v_rows[t] for t in range(n_t)]
            while len(terms) > 1:
                terms = [a + c for a, c in zip(terms[::2], terms[1::2])] + terms[len(terms) & ~1:]
            snew_ref[d, rows, :] = terms[0]
            return tuple(carry[t] + qb_ref[t, d] * s_d for t in range(n_t))

        inter = lax.fori_loop(0, dk, body, tuple(jnp.zeros((SUBLANES, n_seq), F32) for _ in range(n_t)),
                              unroll=STATE_ROW_UNROLL)
        for t in range(n_t):
            chunks[t].append(inter[t])
    for t in range(n_t):
        o_t = o[t] + jnp.concatenate(chunks[t], axis=0)
        ms = jnp.mean(o_t * o_t, axis=0, keepdims=True)
        y_t = o_t * lax.rsqrt(ms + EPS) * _silu(gate[t])
        if norm_col is not None:
            y_t = y_t * norm_col
        y_ref[:, t * n_seq:(t + 1) * n_seq] = y_t.astype(BF16)


def _slabs(x, n_t):
    n_seq = x.shape[1] // n_t
    return [x[:, t * n_seq:(t + 1) * n_seq] for t in range(n_t)]


def _head_rows(ref, dh):
    h = pl.program_id(0)
    return ref[pl.ds(pl.multiple_of(h * dh, 2 * SUBLANES), dh), :]


def _hgrn_head_kernel(q_ref, f_ref, i_ref, g_ref, lbl_ref, hn_ref, s_ref, *rest, layer, n_t):
    snew_ref, y_ref, qb_ref, kb_ref, eb_ref = rest[-5:]
    log_f, k_a = _hgrn_gate(f_ref[...], [lbl_ref[i] for i in range(DEPTH)], layer)
    _head_recurrence(_slabs(_silu(q_ref[...]), n_t), _slabs(k_a, n_t), _slabs(i_ref[...], n_t),
                     _slabs(log_f, n_t), _slabs(g_ref[...], n_t), hn_ref[...], s_ref, snew_ref, y_ref,
                     qb_ref, kb_ref, eb_ref)


def _ret_head_kernel(q_ref, k_ref, v_ref, g_ref, cos_ref, sin_ref, lg_ref, s_ref, *rest, n_t):
    snew_ref, y_ref, qb_ref, kb_ref, eb_ref = rest[-5:]
    half = B_DK // 2

    def rot(x):
        x1, x2 = x[:half], x[half:]
        c, s = cos_ref[...], sin_ref[...]
        return jnp.concatenate([x1 * c - x2 * s, x1 * s + x2 * c], axis=0)

    q = rot(q_ref[...])
    k = rot(k_ref[...]) * (B_DK ** -0.5)
    logd = jnp.broadcast_to(lg_ref[...], (B_DK, q.shape[1] // n_t))
    _head_recurrence(_slabs(q, n_t), _slabs(k, n_t), _slabs(v_ref[...], n_t), [logd] * n_t,
                     _slabs(g_ref[...], n_t), None, s_ref, snew_ref, y_ref, qb_ref, kb_ref, eb_ref)


def _gla_head_kernel(q_ref, k_ref, v_ref, g_ref, r_ref, w2t_ref, gb_ref, gn_ref, s_ref, *rest, n_t):
    snew_ref, y_ref, qb_ref, kb_ref, eb_ref = rest[-5:]
    gk = _dot(_head_rows(w2t_ref, C_DK), r_ref[...].astype(BF16)) + _head_rows(gb_ref, C_DK)
    log_a = jax.nn.log_sigmoid(gk) * (1.0 / C_TAU)
    _head_recurrence(_slabs(_head_rows(q_ref, C_DK) * (C_DK ** -0.5), n_t), _slabs(_head_rows(k_ref, C_DK), n_t),
                     _slabs(_head_rows(v_ref, C_DV), n_t), _slabs(log_a, n_t),
                     _slabs(_head_rows(g_ref, C_DV), n_t), gn_ref[...], s_ref, snew_ref, y_ref,
                     qb_ref, kb_ref, eb_ref)


def _head_call(body, name, projt, segs, extras, extra_specs, state, prev, layer, n_heads, dk, dv, n_t):
    m = projt.shape[1]
    n_seq = m // n_t
    seg_specs = []
    for off, rows, per_head in segs:
        assert off % rows == 0
        if per_head:
            seg_specs.append(pl.BlockSpec((rows, m), functools.partial(lambda i, h: (i + h, 0), off // rows)))
        else:
            seg_specs.append(pl.BlockSpec((rows, m), functools.partial(lambda i, h: (i, 0), off // rows)))
    st_spec = pl.BlockSpec((None, None, dk, dv, n_seq), lambda h: (layer, h, 0, 0, 0))
    in_specs = seg_specs + extra_specs + [st_spec]
    operands = [projt] * len(segs) + extras + [state]
    aliases = {}
    if prev is not None:
        in_specs.append(pl.BlockSpec(memory_space=pl.ANY))
        operands.append(prev)
        aliases = {len(operands) - 1: 0}
    rep_rows = pltpu.VMEM((n_t, dk, SUBLANES, n_seq), F32)
    return pl.pallas_call(
        body,
        grid=(n_heads,),
        in_specs=in_specs,
        out_specs=[st_spec, pl.BlockSpec((dv, m), lambda h: (h, 0))],
        out_shape=[jax.ShapeDtypeStruct(state.shape, F32), jax.ShapeDtypeStruct((n_heads * dv, m), BF16)],
        scratch_shapes=[rep_rows, rep_rows, pltpu.VMEM((dk, SUBLANES, n_seq), F32)],
        input_output_aliases=aliases,
        compiler_params=pltpu.CompilerParams(dimension_semantics=("parallel",), vmem_limit_bytes=VMEM_LIMIT),
        name=name,
    )(*operands)


def _rope_angles(pos):
    half = B_DK // 2
    inv_freq = ROPE_BASE ** (-jnp.arange(half, dtype=F32) / half)
    return pos[:, None] * inv_freq[None, :]


def _rope_tables(pos):
    ang = _rope_angles(pos)
    cos, sin = jnp.cos(ang), jnp.sin(ang)
    cos_h = jnp.concatenate([cos, cos], axis=1)
    sin_h = jnp.concatenate([-sin, sin], axis=1)
    return jnp.tile(cos_h, (1, B_HEADS)), jnp.tile(sin_h, (1, B_HEADS))


def _rope_tables_channel_major(pos, n_seq):
    ang = _rope_angles(pos)
    rep = lambda a: jnp.repeat(a.T, n_seq, axis=1)
    return rep(jnp.cos(ang)), rep(jnp.sin(ang))


def _pad_w_in(w):
    z = lambda n: jnp.zeros(w.shape[:2] + (n,), w.dtype)
    c0 = 2560
    return jnp.concatenate([
        w[..., :c0],
        w[..., c0:c0 + C_QW], z(C_QP - C_QW),
        w[..., c0 + C_QW:c0 + 2 * C_QW], z(C_QP - C_QW),
        w[..., c0 + 2 * C_QW:c0 + 2 * C_QW + 2 * C_VW],
        w[..., c0 + 2 * C_QW + 2 * C_VW:], z(C_RP - C_RANK)], axis=-1)


def kernel(x_prompt, x_sample, state_hgrn, state_ret, state_gla, ffn1_norm, ffn1_w_in, ffn1_w_out, mix_norm, w_in, hgrn_lb_logits, hgrn_norm, gla_w2, gla_b, gla_norm, w_out, ffn2_norm, ffn2_w_in, ffn2_w_out, final_norm):
    bsz, seq, _ = x_prompt.shape
    dbs, dseq, _ = x_sample.shape
    assert seq % ROWS == 0 and dbs % LANES == 0
    assert all(n % tm == 0 for n in (bsz * seq, dbs * dseq) for tm in (TM_IN, TM_OUT))

    f1_in, f1_out = ffn1_w_in.astype(BF16), ffn1_w_out.astype(BF16)
    f2_in, f2_out = ffn2_w_in.astype(BF16), ffn2_w_out.astype(BF16)
    wo = w_out.astype(BF16)
    wmix = _pad_w_in(w_in.astype(BF16))
    stacked = lambda a: a.reshape(DEPTH, 1, -1).astype(F32)
    n1, nm, n2 = stacked(ffn1_norm), stacked(mix_norm), stacked(ffn2_norm)
    nf = final_norm.reshape(1, -1).astype(F32)
    lbl = hgrn_lb_logits.astype(F32)
    w2_pad = jnp.zeros((DEPTH, C_RP, C_QP), BF16).at[:, :C_RANK, :C_QW].set(gla_w2.astype(BF16))
    gb_pad = jnp.zeros((DEPTH, 1, C_QP), F32).at[:, 0, :C_QW].set(gla_b)

    cos_p, sin_p = _rope_tables(jnp.arange(seq, dtype=F32))
    x = x_prompt.reshape(bsz * seq, D_MODEL)
    p_states = []
    for l in range(DEPTH):
        x1, proj = _ffn_proj(x, n1, f1_in, f1_out, nm, wmix, l, False)
        y, sa, sb, sc = _mix_prompt(proj.reshape(bsz, seq, NP), cos_p, sin_p, lbl,
                                    jnp.tile(hgrn_norm[l], A_HEADS).reshape(1, A_W), w2_pad[l], gb_pad[l],
                                    jnp.tile(gla_norm[l], C_HEADS).reshape(1, C_VW), l)
        x = _out_ffn(x1, [y.reshape(bsz * seq, D_MODEL)], wo, n2, f2_in, f2_out, nf, l, False)
        p_states.append((sa, sb, sc))
    y_prompt = x.reshape(bsz, seq, D_MODEL)

    m = dbs * dseq
    x = jnp.transpose(x_sample, (1, 0, 2)).reshape(m, D_MODEL)
    states = [jnp.transpose(s, (0, 2, 3, 4, 1)) for s in (state_hgrn, state_ret, state_gla)]
    new = [None, None, None]
    wmix_t = jnp.swapaxes(wmix, 1, 2)
    cos_s, sin_s = _rope_tables_channel_major(PAST_LEN + jnp.arange(dseq, dtype=F32), dbs)
    lg = jnp.asarray(_LOG_GAMMA, F32).reshape(B_HEADS, 1, 1)
    lbl_col = lbl.reshape(DEPTH, A_W, 1)
    w2_t = jnp.swapaxes(w2_pad, 1, 2)
    gb_col = jnp.swapaxes(gb_pad, 1, 2)
    whole = lambda shape: pl.BlockSpec(shape, lambda h: (0,) * len(shape))
    per_head = lambda off, rows: (off, rows, True)
    segment = lambda off, rows: (off, rows, False)
    for l in range(DEPTH):
        x1, projt = _ffn_proj(x, n1, f1_in, f1_out, nm, wmix_t, l, True)
        new[0], ya = _head_call(
            functools.partial(_hgrn_head_kernel, layer=l, n_t=dseq), "mix_sample_hgrn", projt,
            [per_head(O_AQ, A_DK), per_head(O_AF, A_DK), per_head(O_AI, A_DV), per_head(O_AG, A_DV)],
            [lbl_col, hgrn_norm[l].reshape(A_DV, 1)],
            [pl.BlockSpec((DEPTH, A_DK, 1), lambda h: (0, h, 0)), whole((A_DV, 1))],
            states[0], new[0], l, A_HEADS, A_DK, A_DV, dseq)
        new[1], yb = _head_call(
            functools.partial(_ret_head_kernel, n_t=dseq), "mix_sample_ret", projt,
            [per_head(O_BQ, B_DK), per_head(O_BK, B_DK), per_head(O_BV, B_DV), per_head(O_BG, B_DV)],
            [cos_s, sin_s, lg],
            [whole((B_DK // 2, m)), whole((B_DK // 2, m)), pl.BlockSpec((None, 1, 1), lambda h: (h, 0, 0))],
            states[1], new[1], l, B_HEADS, B_DK, B_DV, dseq)
        new[2], yc = _head_call(
            functools.partial(_gla_head_kernel, n_t=dseq), "mix_sample_gla", projt,
            [segment(O_CQ, C_QP), segment(O_CK, C_QP), segment(O_CV, C_VW), segment(O_CG, C_VW),
             segment(O_CR, C_RP)],
            [w2_t[l], gb_col[l], gla_norm[l].reshape(C_DV, 1)],
            [whole((C_QP, C_RP)), whole((C_QP, 1)), whole((C_DV, 1))],
            states[2], new[2], l, C_HEADS, C_DK, C_DV, dseq)
        x = _out_ffn(x1, [ya, yb, yc], wo, n2, f2_in, f2_out, nf, l, True)
    y_sample = jnp.transpose(x.reshape(dseq, dbs, D_MODEL), (1, 0, 2))
    s_states = [jnp.transpose(s, (0, 4, 1, 2, 3)) for s in new]

    stack = lambda i: jnp.stack([s[i] for s in p_states])
    return (y_prompt, y_sample, stack(0), stack(1), stack(2), s_states[0], s_states[1], s_states[2])
```

```python
import functools
import math

import jax
import jax.numpy as jnp
from jax import lax
from jax.experimental import pallas as pl
from jax.experimental.pallas import tpu as pltpu

F32, BF16 = jnp.float32, jnp.bfloat16

D_MODEL = 1024
DEPTH = 2
A_HEADS, A_DK, A_DV = 4, 64, 64
B_HEADS, B_DK, B_DV = 6, 64, 64
C_HEADS, C_DK, C_DV = 4, 48, 96
C_RANK = 16
C_TAU = 16.0
D_FF = 2816
ROPE_BASE = 10000.0
EPS = 1e-6
PAST_LEN = 16384

A_W = A_HEADS * A_DK
B_W = B_HEADS * B_DK
C_QW = C_HEADS * C_DK
C_VW = C_HEADS * C_DV
LANES = 128
SUBLANES = 8
MXU_DIM = 256
C_QP = 256
C_RP = LANES

O_AQ, O_AF, O_AI, O_AG = 0, 256, 512, 768
O_BQ, O_BK, O_BV, O_BG = 1024, 1408, 1792, 2176
O_CQ, O_CK, O_CV, O_CG, O_CR = 2560, 2816, 3072, 3456, 3840
NP = O_CR + C_RP

ROWS = 128
SEQS_PER_STEP = 4
FACTOR_BLOCK = 64
FACTOR_MAX_EXP = 80.0
STATE_ROW_UNROLL = 4
FF_CHUNK = 256
TM_IN = 512
TM_OUT = 512
VMEM_CAPACITY = 64 * 1024 * 1024
VMEM_LIMIT = VMEM_CAPACITY - 2 * 1024 * 1024


def _dot(a, b):
    return jnp.dot(a, b, preferred_element_type=F32)


def _dot_nt(a, b):
    return lax.dot_general(a, b, (((1,), (1,)), ((), ())), preferred_element_type=F32)


def _dot_tn(a, b):
    return lax.dot_general(a, b, (((0,), (0,)), ((), ())), preferred_element_type=F32)


def _iota(shape, dim):
    return lax.broadcasted_iota(jnp.int32, shape, dim)


def _split3(x):
    hi = x.astype(BF16)
    r = x - hi.astype(F32)
    mid = r.astype(BF16)
    lo = (r - mid.astype(F32)).astype(BF16)
    return hi, mid, lo


def _dot01(mat01, x):
    hi, mid, lo = _split3(x)
    return _dot(mat01, hi) + _dot(mat01, mid) + _dot(mat01, lo)


def _rms(x, w):
    ms = jnp.mean(x * x, axis=-1, keepdims=True)
    return x * lax.rsqrt(ms + EPS) * w


def _silu(x):
    return x * jax.nn.sigmoid(x)


def _swiglu(h_bf, win_ref, wout_ref, act_ref):
    for c in range(D_FF // FF_CHUNK):
        lo = c * FF_CHUNK
        g = _dot(h_bf, win_ref[:, lo:lo + FF_CHUNK])
        u = _dot(h_bf, win_ref[:, D_FF + lo:D_FF + lo + FF_CHUNK])
        act_ref[:, lo:lo + FF_CHUNK] = (_silu(g) * u).astype(BF16)
    return _dot(act_ref[...], wout_ref[...])


def _lower_bound(logit_rows, layer):
    mx = functools.reduce(jnp.maximum, logit_rows)
    ex = [jnp.exp(r - mx) for r in logit_rows]
    tot = functools.reduce(lambda a, c: a + c, ex)
    acc = functools.reduce(lambda a, c: a + c, ex[:layer + 1]) / tot
    return acc - ex[0] / tot


def _hgrn_gate(af, logit_rows, layer):
    if layer == 0:
        return jax.nn.log_sigmoid(af), jax.nn.sigmoid(-af)
    lb = _lower_bound(logit_rows, layer)
    log_f = jnp.logaddexp(jnp.log(lb), jnp.log1p(-lb) + jax.nn.log_sigmoid(af))
    return log_f, (1.0 - lb) * jax.nn.sigmoid(-af)


def _gla_log_decay(low_rank, w2_ref, gb_ref):
    return jax.nn.log_sigmoid(_dot(low_rank.astype(BF16), w2_ref[...]) + gb_ref[...]) * (1.0 / C_TAU)


def _decay_bound(abs_log_decay):
    rows, c = abs_log_decay.shape
    half = FACTOR_BLOCK // 2
    return jnp.max(jnp.sum(abs_log_decay.reshape(rows // half, half, c), axis=1))


def _ffn_proj_kernel(x_ref, n1_ref, win_ref, wout_ref, nm_ref, wmix_ref, *refs, channel_major):
    if channel_major:
        x1_ref, proj_ref, act_ref = refs
    else:
        w2_ref, gb_ref, x1_ref, proj_ref, bound_ref, act_ref = refs
    x = x_ref[...]
    h = _rms(x, n1_ref[...]).astype(BF16)
    x1 = x + 0.5 * _swiglu(h, win_ref, wout_ref, act_ref)
    x1_ref[...] = x1
    hm = _rms(x1, nm_ref[...]).astype(BF16)
    if channel_major:
        proj_ref[...] = _dot_nt(wmix_ref[...], hm)
        return
    proj = _dot(hm, wmix_ref[...])
    proj_ref[...] = proj
    soft = lambda z: jnp.maximum(-z, 0.0) + math.log(2.0)
    gk = _dot(proj[:, O_CR:O_CR + C_RP].astype(BF16), w2_ref[...]) + gb_ref[...]
    bound = jnp.maximum(_decay_bound(soft(proj[:, O_AF:O_AF + A_W])), _decay_bound(soft(gk) * (1.0 / C_TAU)))
    bound_ref[...] = jnp.full(bound_ref.shape, bound, F32)


def _out_ffn_kernel(x_ref, *refs, final, channel_major):
    n_y = 3 if channel_major else 1
    y_refs = refs[:n_y]
    wo_ref, n2_ref, win_ref, wout_ref, nf_ref, o_ref, act_ref = refs[n_y:]
    x2 = x_ref[...]
    if channel_major:
        lo = 0
        for y_ref in y_refs:
            w = y_ref.shape[0]
            x2 = x2 + _dot_tn(y_ref[...], wo_ref[lo:lo + w, :])
            lo += w
    else:
        x2 = x2 + _dot(y_refs[0][...], wo_ref[...])
    h = _rms(x2, n2_ref[...]).astype(BF16)
    x3 = x2 + 0.5 * _swiglu(h, win_ref, wout_ref, act_ref)
    if final:
        x3 = _rms(x3, nf_ref[...])
    o_ref[...] = x3


def _resident(shape, layer=None):
    if layer is None:
        return pl.BlockSpec(shape, lambda i: (0,) * len(shape), pipeline_mode=pl.Buffered(1))
    return pl.BlockSpec((None,) + shape, lambda i: (layer,) + (0,) * len(shape), pipeline_mode=pl.Buffered(1))


def _row_tile(tm, width):
    return pl.BlockSpec((tm, width), lambda i: (i, 0))


def _col_tile(tm, height):
    return pl.BlockSpec((height, tm), lambda i: (0, i))


def _ffn_proj(x, n1, win, wout, nm, wmix, layer, gate_params=None):
    m = x.shape[0]
    tm = TM_IN
    channel_major = gate_params is None
    in_specs = [_row_tile(tm, D_MODEL), _resident((1, D_MODEL), layer), _resident((D_MODEL, 2 * D_FF), layer),
                _resident((D_FF, D_MODEL), layer), _resident((1, D_MODEL), layer)]
    if channel_major:
        in_specs += [_resident((NP, D_MODEL), layer)]
        out_specs = [_row_tile(tm, D_MODEL), _col_tile(tm, NP)]
        out_shape = [jax.ShapeDtypeStruct((m, D_MODEL), F32), jax.ShapeDtypeStruct((NP, m), F32)]
        gate_params = ()
    else:
        in_specs += [_resident((D_MODEL, NP), layer), _resident((C_RP, C_QP), layer), _resident((1, C_QP), layer)]
        out_specs = [_row_tile(tm, D_MODEL), _row_tile(tm, NP),
                     pl.BlockSpec((1, SUBLANES, LANES), lambda i: (i, 0, 0))]
        out_shape = [jax.ShapeDtypeStruct((m, D_MODEL), F32), jax.ShapeDtypeStruct((m, NP), F32),
                     jax.ShapeDtypeStruct((m // tm, SUBLANES, LANES), F32)]
    return pl.pallas_call(
        functools.partial(_ffn_proj_kernel, channel_major=channel_major),
        grid=(m // tm,),
        in_specs=in_specs,
        out_specs=out_specs,
        out_shape=out_shape,
        scratch_shapes=[pltpu.VMEM((tm, D_FF), BF16)],
        compiler_params=pltpu.CompilerParams(dimension_semantics=("parallel",), vmem_limit_bytes=VMEM_LIMIT),
        name="ffn1_proj",
    )(x, n1, win, wout, nm, wmix, *gate_params)


def _out_ffn(x, ys, wo, n2, win, wout, nf, layer, channel_major):
    m = x.shape[0]
    tm = TM_OUT
    y_specs = [_col_tile(tm, y.shape[0]) for y in ys] if channel_major else [_row_tile(tm, D_MODEL)]
    return pl.pallas_call(
        functools.partial(_out_ffn_kernel, final=layer == DEPTH - 1, channel_major=channel_major),
        grid=(m // tm,),
        in_specs=[_row_tile(tm, D_MODEL)] + y_specs + [
            _resident((D_MODEL, D_MODEL), layer), _resident((1, D_MODEL), layer),
            _resident((D_MODEL, 2 * D_FF), layer), _resident((D_FF, D_MODEL), layer), _resident((1, D_MODEL))],
        out_specs=_row_tile(tm, D_MODEL),
        out_shape=jax.ShapeDtypeStruct((m, D_MODEL), F32),
        scratch_shapes=[pltpu.VMEM((tm, D_FF), BF16)],
        compiler_params=pltpu.CompilerParams(dimension_semantics=("parallel",), vmem_limit_bytes=VMEM_LIMIT),
        name="outproj_ffn2",
    )(x, *ys, wo, n2, win, wout, nf)


def _log2(n):
    assert n & (n - 1) == 0
    return n.bit_length() - 1


def _head_mask(width, dh, h):
    lane = _iota((1, width), 1)
    return (lane >= dh * h) & (lane < dh * (h + 1))


def _head_grid_mask(rows, cols, dr, dc, n_heads):
    r = _iota((rows, cols), 0)
    c = _iota((rows, cols), 1)
    m = None
    for h in range(n_heads):
        mh = (r >= dr * h) & (r < dr * (h + 1)) & (c >= dc * h) & (c < dc * (h + 1))
        m = mh if m is None else (m | mh)
    return m


def _seg_mean_sq(o, dh, n_heads):
    w = o.shape[1]
    if w > MXU_DIM and MXU_DIM % dh == 0:
        parts = [_seg_mean_sq(o[:, lo:min(lo + MXU_DIM, w)], dh, (min(lo + MXU_DIM, w) - lo) // dh)
                 for lo in range(0, w, MXU_DIM)]
        return jnp.concatenate(parts, axis=1)
    ones = jnp.where(_head_grid_mask(w, w, dh, dh, n_heads), 1.0, 0.0).astype(BF16)
    return _dot((o * o).astype(BF16), ones) * (1.0 / dh)


def _head_norm(o, dh, n_heads):
    return o * lax.rsqrt(_seg_mean_sq(o, dh, n_heads) + EPS)


def _tile_cumsum(g):
    rows = _iota((ROWS, ROWS), 0)
    cols = _iota((ROWS, ROWS), 1)
    return _dot01(jnp.where(cols <= rows, 1.0, 0.0).astype(BF16), g)


def _block_row(b, size, idx):
    c = b.shape[1]
    if size >= SUBLANES:
        b3 = b.reshape(ROWS // size, size, c)
        return jnp.broadcast_to(b3[:, idx:idx + 1, :], b3.shape).reshape(ROWS, c)
    b8 = b.reshape(ROWS // SUBLANES, SUBLANES, c)
    sub = _iota(b8.shape, 1)
    ref = b8[:, idx:idx + 1, :]
    for blk in range(1, SUBLANES // size):
        lo = blk * size
        ref = jnp.where(sub >= lo, b8[:, lo + idx:lo + idx + 1, :], ref)
    return jnp.broadcast_to(ref, b8.shape).reshape(ROWS, c)


def _stack_heads(x, dh, n_heads, extra_mask=None):
    parts = []
    for h in range(n_heads):
        m = _head_mask(x.shape[1], dh, h)
        if extra_mask is not None:
            m = m & extra_mask
        parts.append(jnp.where(m, x, 0.0))
    return jnp.concatenate(parts, axis=0).astype(BF16)


def _apply_scores(p, v, dv, n_heads):
    pcat = jnp.concatenate([p[h].astype(BF16) for h in range(n_heads)], axis=1)
    return _dot(pcat, _stack_heads(v, dv, n_heads))


def _tree_levels(q, k, b, dk, n_heads, first):
    rows = _iota((ROWS, ROWS), 0)
    cols = _iota((ROWS, ROWS), 1)
    rowc = _iota((ROWS, q.shape[1]), 0)
    total = jnp.zeros((n_heads, ROWS, ROWS), F32)
    m = first
    while m < ROWS:
        ref = _block_row(b, 2 * m, m - 1)
        up = (rowc & (2 * m - 1)) >= m
        e = jnp.exp(jnp.where(up, b - ref, ref - b))
        ql = _stack_heads(q * e, dk, n_heads, extra_mask=up)
        kl = jnp.where(up, 0.0, k * e).astype(BF16)
        sc = _dot_nt(ql, kl).reshape(n_heads, ROWS, ROWS)
        sh = _log2(2 * m)
        total = total + jnp.where(((rows >> sh) == (cols >> sh))[None], sc, 0.0)
        m *= 2
    return total


def _block_deviation(b):
    return b - _block_row(b, FACTOR_BLOCK, FACTOR_BLOCK // 2 - 1)


def _gated_intra(q, k, v, b, dev, dk, dv, n_heads, factorise):
    rows = _iota((ROWS, ROWS), 0)
    cols = _iota((ROWS, ROWS), 1)
    if factorise:
        qf = _stack_heads(q * jnp.exp(dev), dk, n_heads)
        kf = (k * jnp.exp(-dev)).astype(BF16)
        sc = _dot_nt(qf, kf).reshape(n_heads, ROWS, ROWS)
        sh = _log2(FACTOR_BLOCK)
        keep = ((rows >> sh) == (cols >> sh)) & (cols <= rows)
        total = jnp.where(keep[None], sc, 0.0) + _tree_levels(q, k, b, dk, n_heads, FACTOR_BLOCK)
    else:
        sc = _dot_nt(_stack_heads(q, dk, n_heads), k.astype(BF16)).reshape(n_heads, ROWS, ROWS)
        total = jnp.where((rows == cols)[None], sc, 0.0) + _tree_levels(q, k, b, dk, n_heads, 1)
    return _apply_scores(total, v, dv, n_heads)


def _column(row_vec):
    c = row_vec.shape[1]
    return jnp.transpose(jnp.broadcast_to(row_vec, (SUBLANES, c)))[:, 0:1]


def _gated_tile(q, k, v, b, dev, dk, dv, n_heads, factorise, s_ref):
    o = _gated_intra(q, k, v, b, dev, dk, dv, n_heads, factorise)
    s0 = s_ref[...]
    o = o + _dot((q * jnp.exp(b)).astype(BF16), s0.astype(BF16))
    b_last = b[ROWS - 1:ROWS, :]
    khat = (k * jnp.exp(b_last - b)).astype(BF16)
    ds = _dot_tn(khat, v.astype(BF16))
    keep = _head_grid_mask(q.shape[1], v.shape[1], dk, dv, n_heads)
    s_ref[...] = s0 * _column(jnp.exp(b_last)) + jnp.where(keep, ds, 0.0)
    return o


_LOG_GAMMA = tuple(math.log(1.0 - 2.0 ** (-5.0 - h)) for h in range(B_HEADS))


def _per_head_lanes(width, dh, values):
    lane = _iota((1, width), 1)
    out = jnp.zeros((1, width), F32)
    for h, val in enumerate(values):
        out = jnp.where((lane >= dh * h) & (lane < dh * (h + 1)), val, out)
    return out


def _rotary(x, cos, sin_signed):
    w = x.shape[1]
    lane = _iota((1, w), 1)
    first_half = (lane & (B_DK - 1)) < (B_DK // 2)
    partner = jnp.where(first_half, pltpu.roll(x, w - B_DK // 2, 1), pltpu.roll(x, B_DK // 2, 1))
    return x * cos + partner * sin_signed


def _lane_groups(width):
    return [(lo, min(lo + MXU_DIM, width)) for lo in range(0, width, MXU_DIM)]


def _retention_tile(q, k, v, s_ref):
    rows = _iota((ROWS, ROWS), 0)
    cols = _iota((ROWS, ROWS), 1)
    causal = cols <= rows
    dist = (rows - cols).astype(F32)
    tau = _iota((ROWS, 1), 0).astype(F32)
    lg = _per_head_lanes(B_W, B_DK, _LOG_GAMMA)
    qhat = (q * jnp.exp(lg * (tau + 1.0))).astype(BF16)
    khat = (k * jnp.exp(lg * (ROWS - 1.0 - tau))).astype(BF16)
    decay = _column(_per_head_lanes(B_W, B_DK, [math.exp(ROWS * g) for g in _LOG_GAMMA]))
    v_bf = v.astype(BF16)
    outs = []
    for lo, hi in _lane_groups(B_W):
        nh = (hi - lo) // B_DK
        sc = _dot_nt(_stack_heads(q[:, lo:hi], B_DK, nh), k[:, lo:hi].astype(BF16)).reshape(nh, ROWS, ROWS)
        gam = jnp.stack([jnp.where(causal, jnp.exp(dist * g), 0.0) for g in _LOG_GAMMA[lo // B_DK:hi // B_DK]])
        s0 = s_ref[lo:hi, lo:hi]
        outs.append(_apply_scores(sc * gam, v[:, lo:hi], B_DV, nh) + _dot(qhat[:, lo:hi], s0.astype(BF16)))
        ds = _dot_tn(khat[:, lo:hi], v_bf[:, lo:hi])
        keep = _head_grid_mask(hi - lo, hi - lo, B_DK, B_DV, nh)
        s_ref[lo:hi, lo:hi] = s0 * decay[lo:hi, :] + jnp.where(keep, ds, 0.0)
    return jnp.concatenate(outs, axis=1)


def _mix_tile(proj_ref, cos_ref, sin_ref, lbl_ref, hn_ref, w2_ref, gb_ref, gn_ref, sa_ref, sb_ref, sc_ref,
              layer, factorise):
    seg = lambda off, w: proj_ref[:, off:off + w]
    log_f, k_a = _hgrn_gate(seg(O_AF, A_W), [lbl_ref[i:i + 1, :] for i in range(DEPTH)], layer)
    b_a = _tile_cumsum(log_f)
    b_c = _tile_cumsum(_gla_log_decay(seg(O_CR, C_RP), w2_ref, gb_ref))
    dev_a = _block_deviation(b_a) if factorise else None
    dev_c = _block_deviation(b_c) if factorise else None
    o_a = _gated_tile(_silu(seg(O_AQ, A_W)), k_a, seg(O_AI, A_W), b_a, dev_a, A_DK, A_DV, A_HEADS, factorise,
                      sa_ref)
    y_a = _head_norm(o_a, A_DV, A_HEADS) * hn_ref[...] * _silu(seg(O_AG, A_W))
    q_b = _rotary(seg(O_BQ, B_W), cos_ref[...], sin_ref[...])
    k_b = _rotary(seg(O_BK, B_W), cos_ref[...], sin_ref[...]) * (B_DK ** -0.5)
    o_b = _retention_tile(q_b, k_b, seg(O_BV, B_W), sb_ref)
    y_b = _head_norm(o_b, B_DV, B_HEADS) * _silu(seg(O_BG, B_W))
    o_c = _gated_tile(seg(O_CQ, C_QP) * (C_DK ** -0.5), seg(O_CK, C_QP), seg(O_CV, C_VW), b_c, dev_c,
                      C_DK, C_DV, C_HEADS, factorise, sc_ref)
    y_c = _head_norm(o_c, C_DV, C_HEADS) * gn_ref[...] * _silu(seg(O_CG, C_VW))
    return jnp.concatenate([y_a, y_b, y_c], axis=1).astype(BF16)


def _mix_prompt_kernel(bound_ref, proj_ref, cos_ref, sin_ref, lbl_ref, hn_ref, w2_ref, gb_ref, gn_ref,
                       y_ref, sa_ref, sb_ref, sc_ref, sa_acc, sb_acc, sc_acc, *, layer, bounds_per_seq,
                       tiles_per_bound):
    t = pl.program_id(1)

    @pl.when(t == 0)
    def _():
        sa_acc[...] = jnp.zeros_like(sa_acc)
        sb_acc[...] = jnp.zeros_like(sb_acc)
        sc_acc[...] = jnp.zeros_like(sc_acc)

    n_par = proj_ref.shape[0]

    def body(factorise):
        for i in range(n_par):
            y_ref[i] = _mix_tile(proj_ref.at[i], cos_ref, sin_ref, lbl_ref, hn_ref, w2_ref, gb_ref, gn_ref,
                                 sa_acc.at[i], sb_acc.at[i], sc_acc.at[i], layer, factorise)

    safe = None
    for i in range(n_par):
        ok = bound_ref[(pl.program_id(0) * n_par + i) * bounds_per_seq + t // tiles_per_bound] < FACTOR_MAX_EXP
        safe = ok if safe is None else jnp.logical_and(safe, ok)

    @pl.when(safe)
    def _():
        body(True)

    @pl.when(jnp.logical_not(safe))
    def _():
        body(False)

    @pl.when(t == pl.num_programs(1) - 1)
    def _():
        for i in range(n_par):
            for h in range(A_HEADS):
                sa_ref[i, h] = sa_acc[i, A_DK * h:A_DK * (h + 1), A_DV * h:A_DV * (h + 1)]
            for h in range(B_HEADS):
                sb_ref[i, h] = sb_acc[i, B_DK * h:B_DK * (h + 1), B_DV * h:B_DV * (h + 1)]
            for h in range(C_HEADS):
                sc_ref[i, h] = sc_acc[i, C_DK * h:C_DK * (h + 1), C_DV * h:C_DV * (h + 1)]


def _const(shape):
    return pl.BlockSpec(shape, lambda *_: (0,) * len(shape))


def _mix_prompt(bound, proj, cos, sin, lbl, hn, w2, gb, gn, layer):
    bsz, seq, _ = proj.shape
    n_par = SEQS_PER_STEP if bsz % SEQS_PER_STEP == 0 else 1
    bounds_per_seq = bound.shape[0] // bsz
    assert (seq // ROWS) % bounds_per_seq == 0
    tile = lambda w: pl.BlockSpec((n_par, ROWS, w), lambda b, t, _: (b, t, 0))
    tab = pl.BlockSpec((ROWS, B_W), lambda b, t, _: (t, 0))
    st = lambda h, dk, dv: pl.BlockSpec((n_par, h, dk, dv), lambda b, t, _: (b, 0, 0, 0))
    return pl.pallas_call(
        functools.partial(_mix_prompt_kernel, layer=layer, bounds_per_seq=bounds_per_seq,
                          tiles_per_bound=(seq // ROWS) // bounds_per_seq),
        grid_spec=pltpu.PrefetchScalarGridSpec(
            num_scalar_prefetch=1,
            grid=(bsz // n_par, seq // ROWS),
            in_specs=[tile(NP), tab, tab, _const((DEPTH, A_W)), _const((1, A_W)), _const((C_RP, C_QP)),
                      _const((1, C_QP)), _const((1, C_VW))],
            out_specs=[tile(D_MODEL), st(A_HEADS, A_DK, A_DV), st(B_HEADS, B_DK, B_DV),
                       st(C_HEADS, C_DK, C_DV)],
            scratch_shapes=[pltpu.VMEM((n_par, A_W, A_W), F32), pltpu.VMEM((n_par, B_W, B_W), F32),
                            pltpu.VMEM((n_par, C_QP, C_VW), F32)]),
        out_shape=[jax.ShapeDtypeStruct((bsz, seq, D_MODEL), BF16),
                   jax.ShapeDtypeStruct((bsz, A_HEADS, A_DK, A_DV), F32),
                   jax.ShapeDtypeStruct((bsz, B_HEADS, B_DK, B_DV), F32),
                   jax.ShapeDtypeStruct((bsz, C_HEADS, C_DK, C_DV), F32)],
        compiler_params=pltpu.CompilerParams(dimension_semantics=("parallel", "arbitrary"),
                                             vmem_limit_bytes=VMEM_LIMIT),
        name="mix_prompt",
    )(bound, proj, cos, sin, lbl, hn, w2, gb, gn)


def _head_recurrence(q, k, v, logd, gate, norm_col, s_ref, snew_ref, y_ref, qb_ref, kb_ref, eb_ref):
    n_t = len(q)
    dk, n_seq = q[0].shape
    dv = v[0].shape[0]
    b = [logd[0]]
    for t in range(1, n_t):
        b.append(b[-1] + logd[t])
    o = []
    for t in range(n_t):
        acc = jnp.sum(q[t] * k[t], axis=0, keepdims=True) * v[t]
        for s in range(t):
            sc = jnp.sum(q[t] * k[s] * jnp.exp(b[t] - b[s]), axis=0, keepdims=True)
            acc = acc + sc * v[s]
        o.append(acc)
    rep = lambda x: jnp.broadcast_to(x[:, None, :], (dk, SUBLANES, n_seq))
    for t in range(n_t):
        qb_ref[t] = rep(q[t] * jnp.exp(b[t]))
        kb_ref[t] = rep(k[t] * jnp.exp(b[n_t - 1] - b[t]))
    eb_ref[...] = rep(jnp.exp(b[n_t - 1]))
    chunks = [[] for _ in range(n_t)]
    for vc in range(dv // SUBLANES):
        rows = pl.ds(vc * SUBLANES, SUBLANES)
        v_rows = [v[t][vc * SUBLANES:(vc + 1) * SUBLANES, :] for t in range(n_t)]

        def body(d, carry):
            s_d = s_ref[d, rows, :]
            terms = [eb_ref[d] * s_d] + [kb_ref[t, d] * v_rows[t] for t in range(n_t)]
            while len(terms) > 1:
                terms = [a + c for a, c in zip(terms[::2], terms[1::2])] + terms[len(terms) & ~1:]
            snew_ref[d, rows, :] = terms[0]
            return tuple(carry[t] + qb_ref[t, d] * s_d for t in range(n_t))

        inter = lax.fori_loop(0, dk, body, tuple(jnp.zeros((SUBLANES, n_seq), F32) for _ in range(n_t)),
                              unroll=STATE_ROW_UNROLL)
        for t in range(n_t):
            chunks[t].append(inter[t])
    for t in range(n_t):
        o_t = o[t] + jnp.concatenate(chunks[t], axis=0)
        ms = jnp.mean(o_t * o_t, axis=0, keepdims=True)
        y_t = o_t * lax.rsqrt(ms + EPS) * _silu(gate[t])
        if norm_col is not None:
            y_t = y_t * norm_col
        y_ref[:, t * n_seq:(t + 1) * n_seq] = y_t.astype(BF16)


def _slabs(x, n_t):
    n_seq = x.shape[1] // n_t
    return [x[:, t * n_seq:(t + 1) * n_seq] for t in range(n_t)]


def _head_rows(ref, dh):
    h = pl.program_id(0)
    return ref[pl.ds(pl.multiple_of(h * dh, 2 * SUBLANES), dh), :]


def _hgrn_head_kernel(q_ref, f_ref, i_ref, g_ref, lbl_ref, hn_ref, s_ref, *rest, layer, n_t):
    snew_ref, y_ref, qb_ref, kb_ref, eb_ref = rest[-5:]
    log_f, k_a = _hgrn_gate(f_ref[...], [lbl_ref[i] for i in range(DEPTH)], layer)
    _head_recurrence(_slabs(_silu(q_ref[...]), n_t), _slabs(k_a, n_t), _slabs(i_ref[...], n_t),
                     _slabs(log_f, n_t), _slabs(g_ref[...], n_t), hn_ref[...], s_ref, snew_ref, y_ref,
                     qb_ref, kb_ref, eb_ref)


def _ret_head_kernel(q_ref, k_ref, v_ref, g_ref, cos_ref, sin_ref, lg_ref, s_ref, *rest, n_t):
    snew_ref, y_ref, qb_ref, kb_ref, eb_ref = rest[-5:]
    half = B_DK // 2

    def rot(x):
        x1, x2 = x[:half], x[half:]
        c, s = cos_ref[...], sin_ref[...]
        return jnp.concatenate([x1 * c - x2 * s, x1 * s + x2 * c], axis=0)

    q = rot(q_ref[...])
    k = rot(k_ref[...]) * (B_DK ** -0.5)
    logd = jnp.broadcast_to(lg_ref[...], (B_DK, q.shape[1] // n_t))
    _head_recurrence(_slabs(q, n_t), _slabs(k, n_t), _slabs(v_ref[...], n_t), [logd] * n_t,
                     _slabs(g_ref[...], n_t), None, s_ref, snew_ref, y_ref, qb_ref, kb_ref, eb_ref)


def _gla_head_kernel(q_ref, k_ref, v_ref, g_ref, r_ref, w2t_ref, gb_ref, gn_ref, s_ref, *rest, n_t):
    snew_ref, y_ref, qb_ref, kb_ref, eb_ref = rest[-5:]
    gk = _dot(_head_rows(w2t_ref, C_DK), r_ref[...].astype(BF16)) + _head_rows(gb_ref, C_DK)
    log_a = jax.nn.log_sigmoid(gk) * (1.0 / C_TAU)
    _head_recurrence(_slabs(_head_rows(q_ref, C_DK) * (C_DK ** -0.5), n_t), _slabs(_head_rows(k_ref, C_DK), n_t),
                     _slabs(_head_rows(v_ref, C_DV), n_t), _slabs(log_a, n_t),
                     _slabs(_head_rows(g_ref, C_DV), n_t), gn_ref[...], s_ref, snew_ref, y_ref,
                     qb_ref, kb_ref, eb_ref)


def _head_call(body, name, projt, segs, extras, extra_specs, state, prev, layer, n_heads, dk, dv, n_t):
    m = projt.shape[1]
    n_seq = m // n_t
    seg_specs = []
    for off, rows, per_head in segs:
        assert off % rows == 0
        if per_head:
            seg_specs.append(pl.BlockSpec((rows, m), functools.partial(lambda i, h: (i + h, 0), off // rows)))
        else:
            seg_specs.append(pl.BlockSpec((rows, m), functools.partial(lambda i, h: (i, 0), off // rows)))
    st_spec = pl.BlockSpec((None, None, dk, dv, n_seq), lambda h: (layer, h, 0, 0, 0))
    in_specs = seg_specs + extra_specs + [st_spec, pl.BlockSpec(memory_space=pl.ANY)]
    operands = [projt] * len(segs) + extras + [state, prev]
    aliases = {len(operands) - 1: 0}
    rep_rows = pltpu.VMEM((n_t, dk, SUBLANES, n_seq), F32)
    return pl.pallas_call(
        body,
        grid=(n_heads,),
        in_specs=in_specs,
        out_specs=[st_spec, pl.BlockSpec((dv, m), lambda h: (h, 0))],
        out_shape=[jax.ShapeDtypeStruct(state.shape, F32), jax.ShapeDtypeStruct((n_heads * dv, m), BF16)],
        scratch_shapes=[rep_rows, rep_rows, pltpu.VMEM((dk, SUBLANES, n_seq), F32)],
        input_output_aliases=aliases,
        compiler_params=pltpu.CompilerParams(dimension_semantics=("parallel",), vmem_limit_bytes=VMEM_LIMIT),
        name=name,
    )(*operands)


def _rope_angles(pos):
    half = B_DK // 2
    inv_freq = ROPE_BASE ** (-jnp.arange(half, dtype=F32) / half)
    return pos[:, None] * inv_freq[None, :]


def _rope_tables(pos):
    ang = _rope_angles(pos)
    cos, sin = jnp.cos(ang), jnp.sin(ang)
    cos_h = jnp.concatenate([cos, cos], axis=1)
    sin_h = jnp.concatenate([-sin, sin], axis=1)
    return jnp.tile(cos_h, (1, B_HEADS)), jnp.tile(sin_h, (1, B_HEADS))


def _rope_tables_channel_major(pos, n_seq):
    ang = _rope_angles(pos)
    rep = lambda a: jnp.repeat(a.T, n_seq, axis=1)
    return rep(jnp.cos(ang)), rep(jnp.sin(ang))


def _pad_w_in(w):
    z = lambda n: jnp.zeros(w.shape[:2] + (n,), w.dtype)
    c0 = 2560
    return jnp.concatenate([
        w[..., :c0],
        w[..., c0:c0 + C_QW], z(C_QP - C_QW),
        w[..., c0 + C_QW:c0 + 2 * C_QW], z(C_QP - C_QW),
        w[..., c0 + 2 * C_QW:c0 + 2 * C_QW + 2 * C_VW],
        w[..., c0 + 2 * C_QW + 2 * C_VW:], z(C_RP - C_RANK)], axis=-1)


def kernel(x_prompt, x_sample, state_hgrn, state_ret, state_gla, ffn1_norm, ffn1_w_in, ffn1_w_out, mix_norm, w_in, hgrn_lb_logits, hgrn_norm, gla_w2, gla_b, gla_norm, w_out, ffn2_norm, ffn2_w_in, ffn2_w_out, final_norm):
    bsz, seq, _ = x_prompt.shape
    dbs, dseq, _ = x_sample.shape
    assert seq % ROWS == 0 and seq % TM_IN == 0 and dbs % LANES == 0
    assert all(n % tm == 0 for n in (bsz * seq, dbs * dseq) for tm in (TM_IN, TM_OUT))

    f1_in, f1_out = ffn1_w_in.astype(BF16), ffn1_w_out.astype(BF16)
    f2_in, f2_out = ffn2_w_in.astype(BF16), ffn2_w_out.astype(BF16)
    wo = w_out.astype(BF16)
    wmix = _pad_w_in(w_in.astype(BF16))
    stacked = lambda a: a.reshape(DEPTH, 1, -1).astype(F32)
    n1, nm, n2 = stacked(ffn1_norm), stacked(mix_norm), stacked(ffn2_norm)
    nf = final_norm.reshape(1, -1).astype(F32)
    lbl = hgrn_lb_logits.astype(F32)
    w2_pad = jnp.zeros((DEPTH, C_RP, C_QP), BF16).at[:, :C_RANK, :C_QW].set(gla_w2.astype(BF16))
    gb_pad = jnp.zeros((DEPTH, 1, C_QP), F32).at[:, 0, :C_QW].set(gla_b)

    cos_p, sin_p = _rope_tables(jnp.arange(seq, dtype=F32))
    x = x_prompt.reshape(bsz * seq, D_MODEL)
    p_states = []
    for l in range(DEPTH):
        x1, proj, bound = _ffn_proj(x, n1, f1_in, f1_out, nm, wmix, l, (w2_pad, gb_pad))
        y, sa, sb, sc = _mix_prompt(bound[:, 0, 0], proj.reshape(bsz, seq, NP), cos_p, sin_p, lbl,
                                    jnp.tile(hgrn_norm[l], A_HEADS).reshape(1, A_W), w2_pad[l], gb_pad[l],
                                    jnp.tile(gla_norm[l], C_HEADS).reshape(1, C_VW), l)
        x = _out_ffn(x1, [y.reshape(bsz * seq, D_MODEL)], wo, n2, f2_in, f2_out, nf, l, False)
        p_states.append((sa, sb, sc))
    y_prompt = x.reshape(bsz, seq, D_MODEL)

    m = dbs * dseq
    x = jnp.transpose(x_sample, (1, 0, 2)).reshape(m, D_MODEL)
    states = [jnp.transpose(s, (0, 2, 3, 4, 1)) for s in (state_hgrn, state_ret, state_gla)]
    new = [jnp.zeros(s.shape, F32) for s in states]
    wmix_t = jnp.swapaxes(wmix, 1, 2)
    cos_s, sin_s = _rope_tables_channel_major(PAST_LEN + jnp.arange(dseq, dtype=F32), dbs)
    lg = jnp.asarray(_LOG_GAMMA, F32).reshape(B_HEADS, 1, 1)
    lbl_col = lbl.reshape(DEPTH, A_W, 1)
    w2_t = jnp.swapaxes(w2_pad, 1, 2)
    gb_col = jnp.swapaxes(gb_pad, 1, 2)
    whole = lambda shape: pl.BlockSpec(shape, lambda h: (0,) * len(shape))
    per_head = lambda off, rows: (off, rows, True)
    segment = lambda off, rows: (off, rows, False)
    for l in range(DEPTH):
        x1, projt = _ffn_proj(x, n1, f1_in, f1_out, nm, wmix_t, l)
        new[0], ya = _head_call(
            functools.partial(_hgrn_head_kernel, layer=l, n_t=dseq), "mix_sample_hgrn", projt,
            [per_head(O_AQ, A_DK), per_head(O_AF, A_DK), per_head(O_AI, A_DV), per_head(O_AG, A_DV)],
            [lbl_col, hgrn_norm[l].reshape(A_DV, 1)],
            [pl.BlockSpec((DEPTH, A_DK, 1), lambda h: (0, h, 0)), whole((A_DV, 1))],
            states[0], new[0], l, A_HEADS, A_DK, A_DV, dseq)
        new[1], yb = _head_call(
            functools.partial(_ret_head_kernel, n_t=dseq), "mix_sample_ret", projt,
            [per_head(O_BQ, B_DK), per_head(O_BK, B_DK), per_head(O_BV, B_DV), per_head(O_BG, B_DV)],
            [cos_s, sin_s, lg],
            [whole((B_DK // 2, m)), whole((B_DK // 2, m)), pl.BlockSpec((None, 1, 1), lambda h: (h, 0, 0))],
            states[1], new[1], l, B_HEADS, B_DK, B_DV, dseq)
        new[2], yc = _head_call(
            functools.partial(_gla_head_kernel, n_t=dseq), "mix_sample_gla", projt,
            [segment(O_CQ, C_QP), segment(O_CK, C_QP), segment(O_CV, C_VW), segment(O_CG, C_VW),
             segment(O_CR, C_RP)],
            [w2_t[l], gb_col[l], gla_norm[l].reshape(C_DV, 1)],
            [whole((C_QP, C_RP)), whole((C_QP, 1)), whole((C_DV, 1))],
            states[2], new[2], l, C_HEADS, C_DK, C_DV, dseq)
        x = _out_ffn(x1, [ya, yb, yc], wo, n2, f2_in, f2_out, nf, l, True)
    y_sample = jnp.transpose(x.reshape(dseq, dbs, D_MODEL), (1, 0, 2))
    s_states = [jnp.transpose(s, (0, 4, 1, 2, 3)) for s in new]

    stack = lambda i: jnp.stack([s[i] for s in p_states])
    return (y_prompt, y_sample, stack(0), stack(1), stack(2), s_states[0], s_states[1], s_states[2])
```

```python
import functools
import math

import jax
import jax.numpy as jnp
from jax import lax
from jax.experimental import pallas as pl
from jax.experimental.pallas import tpu as pltpu

F32, BF16 = jnp.float32, jnp.bfloat16

D_MODEL = 1024
DEPTH = 2
A_HEADS, A_DK, A_DV = 4, 64, 64
B_HEADS, B_DK, B_DV = 6, 64, 64
C_HEADS, C_DK, C_DV = 4, 48, 96
C_RANK = 16
C_TAU = 16.0
D_FF = 2816
ROPE_BASE = 10000.0
EPS = 1e-6
PAST_LEN = 16384

A_W = A_HEADS * A_DK
B_W = B_HEADS * B_DK
C_QW = C_HEADS * C_DK
C_VW = C_HEADS * C_DV
LANES = 128
SUBLANES = 8
MXU_DIM = 256
C_QP = 256
C_RP = LANES

O_AQ, O_AF, O_AI, O_AG = 0, 256, 512, 768
O_BQ, O_BK, O_BV, O_BG = 1024, 1408, 1792, 2176
O_CQ, O_CK, O_CV, O_CG, O_CR = 2560, 2816, 3072, 3456, 3840
NP = O_CR + C_RP

ROWS = 128
SEQS_PER_STEP = 4
FACTOR_BLOCK = 64
FACTOR_MAX_EXP = 80.0
STATE_ROW_UNROLL = 4
FF_CHUNK = 256
TM_IN = 512
TM_OUT = 512
VMEM_CAPACITY = 64 * 1024 * 1024
VMEM_LIMIT = VMEM_CAPACITY - 2 * 1024 * 1024


def _dot(a, b):
    return jnp.dot(a, b, preferred_element_type=F32)


def _dot_nt(a, b):
    return lax.dot_general(a, b, (((1,), (1,)), ((), ())), preferred_element_type=F32)


def _dot_tn(a, b):
    return lax.dot_general(a, b, (((0,), (0,)), ((), ())), preferred_element_type=F32)


def _iota(shape, dim):
    return lax.broadcasted_iota(jnp.int32, shape, dim)


def _split3(x):
    hi = x.astype(BF16)
    r = x - hi.astype(F32)
    mid = r.astype(BF16)
    lo = (r - mid.astype(F32)).astype(BF16)
    return hi, mid, lo


def _dot01(mat01, x):
    hi, mid, lo = _split3(x)
    return _dot(mat01, hi) + _dot(mat01, mid) + _dot(mat01, lo)


def _rms(x, w):
    ms = jnp.mean(x * x, axis=-1, keepdims=True)
    return x * lax.rsqrt(ms + EPS) * w


def _silu(x):
    return x * jax.nn.sigmoid(x)


def _swiglu(h_bf, win_ref, wout_ref, act_ref):
    for c in range(D_FF // FF_CHUNK):
        lo = c * FF_CHUNK
        g = _dot(h_bf, win_ref[:, lo:lo + FF_CHUNK])
        u = _dot(h_bf, win_ref[:, D_FF + lo:D_FF + lo + FF_CHUNK])
        act_ref[:, lo:lo + FF_CHUNK] = (_silu(g) * u).astype(BF16)
    return _dot(act_ref[...], wout_ref[...])


def _lower_bound(logit_rows, layer):
    mx = functools.reduce(jnp.maximum, logit_rows)
    ex = [jnp.exp(r - mx) for r in logit_rows]
    tot = functools.reduce(lambda a, c: a + c, ex)
    acc = functools.reduce(lambda a, c: a + c, ex[:layer + 1]) / tot
    return acc - ex[0] / tot


def _hgrn_gate(af, logit_rows, layer):
    if layer == 0:
        return jax.nn.log_sigmoid(af), jax.nn.sigmoid(-af)
    lb = _lower_bound(logit_rows, layer)
    log_f = jnp.logaddexp(jnp.log(lb), jnp.log1p(-lb) + jax.nn.log_sigmoid(af))
    return log_f, (1.0 - lb) * jax.nn.sigmoid(-af)


def _gla_log_decay(low_rank, w2_ref, gb_ref):
    return jax.nn.log_sigmoid(_dot(low_rank.astype(BF16), w2_ref[...]) + gb_ref[...]) * (1.0 / C_TAU)


def _decay_bound(abs_log_decay):
    rows, c = abs_log_decay.shape
    half = FACTOR_BLOCK // 2
    return jnp.max(jnp.sum(abs_log_decay.reshape(rows // half, half, c), axis=1))


def _ffn_proj_kernel(x_ref, n1_ref, win_ref, wout_ref, nm_ref, wmix_ref, *refs, channel_major):
    if channel_major:
        x1_ref, proj_ref, act_ref = refs
    else:
        w2_ref, gb_ref, x1_ref, proj_ref, bound_ref, act_ref = refs
    x = x_ref[...]
    h = _rms(x, n1_ref[...]).astype(BF16)
    x1 = x + 0.5 * _swiglu(h, win_ref, wout_ref, act_ref)
    x1_ref[...] = x1
    hm = _rms(x1, nm_ref[...]).astype(BF16)
    if channel_major:
        proj_ref[...] = _dot_nt(wmix_ref[...], hm)
        return
    proj = _dot(hm, wmix_ref[...])
    proj_ref[...] = proj
    soft = lambda z: jnp.maximum(-z, 0.0) + math.log(2.0)
    gk = _dot(proj[:, O_CR:O_CR + C_RP].astype(BF16), w2_ref[...]) + gb_ref[...]
    bound = jnp.maximum(_decay_bound(soft(proj[:, O_AF:O_AF + A_W])), _decay_bound(soft(gk) * (1.0 / C_TAU)))
    bound_ref[...] = jnp.full(bound_ref.shape, bound, F32)


def _out_ffn_kernel(x_ref, *refs, final, channel_major):
    n_y = 3 if channel_major else 1
    y_refs = refs[:n_y]
    wo_ref, n2_ref, win_ref, wout_ref, nf_ref, o_ref, act_ref = refs[n_y:]
    x2 = x_ref[...]
    if channel_major:
        lo = 0
        for y_ref in y_refs:
            w = y_ref.shape[0]
            x2 = x2 + _dot_tn(y_ref[...], wo_ref[lo:lo + w, :])
            lo += w
    else:
        x2 = x2 + _dot(y_refs[0][...], wo_ref[...])
    h = _rms(x2, n2_ref[...]).astype(BF16)
    x3 = x2 + 0.5 * _swiglu(h, win_ref, wout_ref, act_ref)
    if final:
        x3 = _rms(x3, nf_ref[...])
    o_ref[...] = x3


def _resident(shape, layer=None):
    if layer is None:
        return pl.BlockSpec(shape, lambda i: (0,) * len(shape), pipeline_mode=pl.Buffered(1))
    return pl.BlockSpec((None,) + shape, lambda i: (layer,) + (0,) * len(shape), pipeline_mode=pl.Buffered(1))


def _row_tile(tm, width):
    return pl.BlockSpec((tm, width), lambda i: (i, 0))


def _col_tile(tm, height):
    return pl.BlockSpec((height, tm), lambda i: (0, i))


def _ffn_proj(x, n1, win, wout, nm, wmix, layer, gate_params=None):
    m = x.shape[0]
    tm = TM_IN
    channel_major = gate_params is None
    in_specs = [_row_tile(tm, D_MODEL), _resident((1, D_MODEL), layer), _resident((D_MODEL, 2 * D_FF), layer),
                _resident((D_FF, D_MODEL), layer), _resident((1, D_MODEL), layer)]
    if channel_major:
        in_specs += [_resident((NP, D_MODEL), layer)]
        out_specs = [_row_tile(tm, D_MODEL), _col_tile(tm, NP)]
        out_shape = [jax.ShapeDtypeStruct((m, D_MODEL), F32), jax.ShapeDtypeStruct((NP, m), F32)]
        gate_params = ()
    else:
        in_specs += [_resident((D_MODEL, NP), layer), _resident((C_RP, C_QP), layer), _resident((1, C_QP), layer)]
        out_specs = [_row_tile(tm, D_MODEL), _row_tile(tm, NP),
                     pl.BlockSpec((1, SUBLANES, LANES), lambda i: (i, 0, 0))]
        out_shape = [jax.ShapeDtypeStruct((m, D_MODEL), F32), jax.ShapeDtypeStruct((m, NP), F32),
                     jax.ShapeDtypeStruct((m // tm, SUBLANES, LANES), F32)]
    return pl.pallas_call(
        functools.partial(_ffn_proj_kernel, channel_major=channel_major),
        grid=(m // tm,),
        in_specs=in_specs,
        out_specs=out_specs,
        out_shape=out_shape,
        scratch_shapes=[pltpu.VMEM((tm, D_FF), BF16)],
        compiler_params=pltpu.CompilerParams(dimension_semantics=("parallel",), vmem_limit_bytes=VMEM_LIMIT),
        name="ffn1_proj",
    )(x, n1, win, wout, nm, wmix, *gate_params)


def _out_ffn(x, ys, wo, n2, win, wout, nf, layer, channel_major):
    m = x.shape[0]
    tm = TM_OUT
    y_specs = [_col_tile(tm, y.shape[0]) for y in ys] if channel_major else [_row_tile(tm, D_MODEL)]
    return pl.pallas_call(
        functools.partial(_out_ffn_kernel, final=layer == DEPTH - 1, channel_major=channel_major),
        grid=(m // tm,),
        in_specs=[_row_tile(tm, D_MODEL)] + y_specs + [
            _resident((D_MODEL, D_MODEL), layer), _resident((1, D_MODEL), layer),
            _resident((D_MODEL, 2 * D_FF), layer), _resident((D_FF, D_MODEL), layer), _resident((1, D_MODEL))],
        out_specs=_row_tile(tm, D_MODEL),
        out_shape=jax.ShapeDtypeStruct((m, D_MODEL), F32),
        scratch_shapes=[pltpu.VMEM((tm, D_FF), BF16)],
        compiler_params=pltpu.CompilerParams(dimension_semantics=("parallel",), vmem_limit_bytes=VMEM_LIMIT),
        name="outproj_ffn2",
    )(x, *ys, wo, n2, win, wout, nf)


def _log2(n):
    assert n & (n - 1) == 0
    return n.bit_length() - 1


def _head_mask(width, dh, h):
    lane = _iota((1, width), 1)
    return (lane >= dh * h) & (lane < dh * (h + 1))


def _head_grid_mask(rows, cols, dr, dc, n_heads):
    r = _iota((rows, cols), 0)
    c = _iota((rows, cols), 1)
    m = None
    for h in range(n_heads):
        mh = (r >= dr * h) & (r < dr * (h + 1)) & (c >= dc * h) & (c < dc * (h + 1))
        m = mh if m is None else (m | mh)
    return m


def _seg_mean_sq(o, dh, n_heads):
    w = o.shape[1]
    if w > MXU_DIM and MXU_DIM % dh == 0:
        parts = [_seg_mean_sq(o[:, lo:min(lo + MXU_DIM, w)], dh, (min(lo + MXU_DIM, w) - lo) // dh)
                 for lo in range(0, w, MXU_DIM)]
        return jnp.concatenate(parts, axis=1)
    ones = jnp.where(_head_grid_mask(w, w, dh, dh, n_heads), 1.0, 0.0).astype(BF16)
    return _dot((o * o).astype(BF16), ones) * (1.0 / dh)


def _head_norm(o, dh, n_heads):
    return o * lax.rsqrt(_seg_mean_sq(o, dh, n_heads) + EPS)


def _tile_cumsum(g):
    rows = _iota((ROWS, ROWS), 0)
    cols = _iota((ROWS, ROWS), 1)
    return _dot01(jnp.where(cols <= rows, 1.0, 0.0).astype(BF16), g)


def _block_row(b, size, idx):
    c = b.shape[1]
    if size >= SUBLANES:
        b3 = b.reshape(ROWS // size, size, c)
        return jnp.broadcast_to(b3[:, idx:idx + 1, :], b3.shape).reshape(ROWS, c)
    b8 = b.reshape(ROWS // SUBLANES, SUBLANES, c)
    sub = _iota(b8.shape, 1)
    ref = b8[:, idx:idx + 1, :]
    for blk in range(1, SUBLANES // size):
        lo = blk * size
        ref = jnp.where(sub >= lo, b8[:, lo + idx:lo + idx + 1, :], ref)
    return jnp.broadcast_to(ref, b8.shape).reshape(ROWS, c)


def _stack_heads(x, dh, n_heads, extra_mask=None):
    parts = []
    for h in range(n_heads):
        m = _head_mask(x.shape[1], dh, h)
        if extra_mask is not None:
            m = m & extra_mask
        parts.append(jnp.where(m, x, 0.0))
    return jnp.concatenate(parts, axis=0).astype(BF16)


def _apply_scores(p, v, dv, n_heads):
    pcat = jnp.concatenate([p[h].astype(BF16) for h in range(n_heads)], axis=1)
    return _dot(pcat, _stack_heads(v, dv, n_heads))


def _tree_levels(q, k, b, dk, n_heads, first):
    rows = _iota((ROWS, ROWS), 0)
    cols = _iota((ROWS, ROWS), 1)
    rowc = _iota((ROWS, q.shape[1]), 0)
    total = jnp.zeros((n_heads, ROWS, ROWS), F32)
    m = first
    while m < ROWS:
        ref = _block_row(b, 2 * m, m - 1)
        up = (rowc & (2 * m - 1)) >= m
        e = jnp.exp(jnp.where(up, b - ref, ref - b))
        ql = _stack_heads(q * e, dk, n_heads, extra_mask=up)
        kl = jnp.where(up, 0.0, k * e).astype(BF16)
        sc = _dot_nt(ql, kl).reshape(n_heads, ROWS, ROWS)
        sh = _log2(2 * m)
        total = total + jnp.where(((rows >> sh) == (cols >> sh))[None], sc, 0.0)
        m *= 2
    return total


def _block_deviation(b):
    return b - _block_row(b, FACTOR_BLOCK, FACTOR_BLOCK // 2 - 1)


def _gated_intra(q, k, v, b, dev, dk, dv, n_heads, factorise):
    rows = _iota((ROWS, ROWS), 0)
    cols = _iota((ROWS, ROWS), 1)
    if factorise:
        qf = _stack_heads(q * jnp.exp(dev), dk, n_heads)
        kf = (k * jnp.exp(-dev)).astype(BF16)
        sc = _dot_nt(qf, kf).reshape(n_heads, ROWS, ROWS)
        sh = _log2(FACTOR_BLOCK)
        keep = ((rows >> sh) == (cols >> sh)) & (cols <= rows)
        total = jnp.where(keep[None], sc, 0.0) + _tree_levels(q, k, b, dk, n_heads, FACTOR_BLOCK)
    else:
        sc = _dot_nt(_stack_heads(q, dk, n_heads), k.astype(BF16)).reshape(n_heads, ROWS, ROWS)
        total = jnp.where((rows == cols)[None], sc, 0.0) + _tree_levels(q, k, b, dk, n_heads, 1)
    return _apply_scores(total, v, dv, n_heads)


def _column(row_vec):
    c = row_vec.shape[1]
    return jnp.transpose(jnp.broadcast_to(row_vec, (SUBLANES, c)))[:, 0:1]


def _gated_tile(q, k, v, b, dev, dk, dv, n_heads, factorise, s_ref):
    o = _gated_intra(q, k, v, b, dev, dk, dv, n_heads, factorise)
    s0 = s_ref[...]
    o = o + _dot((q * jnp.exp(b)).astype(BF16), s0.astype(BF16))
    b_last = b[ROWS - 1:ROWS, :]
    khat = (k * jnp.exp(b_last - b)).astype(BF16)
    ds = _dot_tn(khat, v.astype(BF16))
    keep = _head_grid_mask(q.shape[1], v.shape[1], dk, dv, n_heads)
    s_ref[...] = s0 * _column(jnp.exp(b_last)) + jnp.where(keep, ds, 0.0)
    return o


_LOG_GAMMA = tuple(math.log(1.0 - 2.0 ** (-5.0 - h)) for h in range(B_HEADS))


def _per_head_lanes(width, dh, values):
    lane = _iota((1, width), 1)
    out = jnp.zeros((1, width), F32)
    for h, val in enumerate(values):
        out = jnp.where((lane >= dh * h) & (lane < dh * (h + 1)), val, out)
    return out


def _rotary(x, cos, sin_signed):
    w = x.shape[1]
    lane = _iota((1, w), 1)
    first_half = (lane & (B_DK - 1)) < (B_DK // 2)
    partner = jnp.where(first_half, pltpu.roll(x, w - B_DK // 2, 1), pltpu.roll(x, B_DK // 2, 1))
    return x * cos + partner * sin_signed


def _lane_groups(width):
    return [(lo, min(lo + MXU_DIM, width)) for lo in range(0, width, MXU_DIM)]


def _retention_tile(q, k, v, s_ref):
    rows = _iota((ROWS, ROWS), 0)
    cols = _iota((ROWS, ROWS), 1)
    causal = cols <= rows
    dist = (rows - cols).astype(F32)
    tau = _iota((ROWS, 1), 0).astype(F32)
    lg = _per_head_lanes(B_W, B_DK, _LOG_GAMMA)
    qhat = (q * jnp.exp(lg * (tau + 1.0))).astype(BF16)
    khat = (k * jnp.exp(lg * (ROWS - 1.0 - tau))).astype(BF16)
    decay = _column(_per_head_lanes(B_W, B_DK, [math.exp(ROWS * g) for g in _LOG_GAMMA]))
    v_bf = v.astype(BF16)
    outs = []
    for lo, hi in _lane_groups(B_W):
        nh = (hi - lo) // B_DK
        sc = _dot_nt(_stack_heads(q[:, lo:hi], B_DK, nh), k[:, lo:hi].astype(BF16)).reshape(nh, ROWS, ROWS)
        gam = jnp.stack([jnp.where(causal, jnp.exp(dist * g), 0.0) for g in _LOG_GAMMA[lo // B_DK:hi // B_DK]])
        s0 = s_ref[lo:hi, lo:hi]
        outs.append(_apply_scores(sc * gam, v[:, lo:hi], B_DV, nh) + _dot(qhat[:, lo:hi], s0.astype(BF16)))
        ds = _dot_tn(khat[:, lo:hi], v_bf[:, lo:hi])
        keep = _head_grid_mask(hi - lo, hi - lo, B_DK, B_DV, nh)
        s_ref[lo:hi, lo:hi] = s0 * decay[lo:hi, :] + jnp.where(keep, ds, 0.0)
    return jnp.concatenate(outs, axis=1)


def _mix_tile(proj_ref, cos_ref, sin_ref, lbl_ref, hn_ref, w2_ref, gb_ref, gn_ref, sa_ref, sb_ref, sc_ref,
              layer, factorise):
    seg = lambda off, w: proj_ref[:, off:off + w]
    log_f, k_a = _hgrn_gate(seg(O_AF, A_W), [lbl_ref[i:i + 1, :] for i in range(DEPTH)], layer)
    b_a = _tile_cumsum(log_f)
    b_c = _tile_cumsum(_gla_log_decay(seg(O_CR, C_RP), w2_ref, gb_ref))
    dev_a = _block_deviation(b_a) if factorise else None
    dev_c = _block_deviation(b_c) if factorise else None
    o_a = _gated_tile(_silu(seg(O_AQ, A_W)), k_a, seg(O_AI, A_W), b_a, dev_a, A_DK, A_DV, A_HEADS, factorise,
                      sa_ref)
    y_a = _head_norm(o_a, A_DV, A_HEADS) * hn_ref[...] * _silu(seg(O_AG, A_W))
    q_b = _rotary(seg(O_BQ, B_W), cos_ref[...], sin_ref[...])
    k_b = _rotary(seg(O_BK, B_W), cos_ref[...], sin_ref[...]) * (B_DK ** -0.5)
    o_b = _retention_tile(q_b, k_b, seg(O_BV, B_W), sb_ref)
    y_b = _head_norm(o_b, B_DV, B_HEADS) * _silu(seg(O_BG, B_W))
    o_c = _gated_tile(seg(O_CQ, C_QP) * (C_DK ** -0.5), seg(O_CK, C_QP), seg(O_CV, C_VW), b_c, dev_c,
                      C_DK, C_DV, C_HEADS, factorise, sc_ref)
    y_c = _head_norm(o_c, C_DV, C_HEADS) * gn_ref[...] * _silu(seg(O_CG, C_VW))
    return jnp.concatenate([y_a, y_b, y_c], axis=1).astype(BF16)


def _mix_prompt_kernel(bound_ref, proj_ref, cos_ref, sin_ref, lbl_ref, hn_ref, w2_ref, gb_ref, gn_ref,
                       y_ref, sa_ref, sb_ref, sc_ref, sa_acc, sb_acc, sc_acc, *, layer, bounds_per_seq,
                       tiles_per_bound):
    t = pl.program_id(1)

    @pl.when(t == 0)
    def _():
        sa_acc[...] = jnp.zeros_like(sa_acc)
        sb_acc[...] = jnp.zeros_like(sb_acc)
        sc_acc[...] = jnp.zeros_like(sc_acc)

    n_par = proj_ref.shape[0]

    def body(factorise):
        for i in range(n_par):
            y_ref[i] = _mix_tile(proj_ref.at[i], cos_ref, sin_ref, lbl_ref, hn_ref, w2_ref, gb_ref, gn_ref,
                                 sa_acc.at[i], sb_acc.at[i], sc_acc.at[i], layer, factorise)

    safe = None
    for i in range(n_par):
        ok = bound_ref[(pl.program_id(0) * n_par + i) * bounds_per_seq + t // tiles_per_bound] < FACTOR_MAX_EXP
        safe = ok if safe is None else jnp.logical_and(safe, ok)

    @pl.when(safe)
    def _():
        body(True)

    @pl.when(jnp.logical_not(safe))
    def _():
        body(False)

    @pl.when(t == pl.num_programs(1) - 1)
    def _():
        for i in range(n_par):
            for h in range(A_HEADS):
                sa_ref[i, h] = sa_acc[i, A_DK * h:A_DK * (h + 1), A_DV * h:A_DV * (h + 1)]
            for h in range(B_HEADS):
                sb_ref[i, h] = sb_acc[i, B_DK * h:B_DK * (h + 1), B_DV * h:B_DV * (h + 1)]
            for h in range(C_HEADS):
                sc_ref[i, h] = sc_acc[i, C_DK * h:C_DK * (h + 1), C_DV * h:C_DV * (h + 1)]


def _const(shape):
    return pl.BlockSpec(shape, lambda *_: (0,) * len(shape))


def _mix_prompt(bound, proj, cos, sin, lbl, hn, w2, gb, gn, layer):
    bsz, seq, _ = proj.shape
    n_par = SEQS_PER_STEP if bsz % SEQS_PER_STEP == 0 else 1
    bounds_per_seq = bound.shape[0] // bsz
    assert (seq // ROWS) % bounds_per_seq == 0
    tile = lambda w: pl.BlockSpec((n_par, ROWS, w), lambda b, t, _: (b, t, 0))
    tab = pl.BlockSpec((ROWS, B_W), lambda b, t, _: (t, 0))
    st = lambda h, dk, dv: pl.BlockSpec((n_par, h, dk, dv), lambda b, t, _: (b, 0, 0, 0))
    return pl.pallas_call(
        functools.partial(_mix_prompt_kernel, layer=layer, bounds_per_seq=bounds_per_seq,
                          tiles_per_bound=(seq // ROWS) // bounds_per_seq),
        grid_spec=pltpu.PrefetchScalarGridSpec(
            num_scalar_prefetch=1,
            grid=(bsz // n_par, seq // ROWS),
            in_specs=[tile(NP), tab, tab, _const((DEPTH, A_W)), _const((1, A_W)), _const((C_RP, C_QP)),
                      _const((1, C_QP)), _const((1, C_VW))],
            out_specs=[tile(D_MODEL), st(A_HEADS, A_DK, A_DV), st(B_HEADS, B_DK, B_DV),
                       st(C_HEADS, C_DK, C_DV)],
            scratch_shapes=[pltpu.VMEM((n_par, A_W, A_W), F32), pltpu.VMEM((n_par, B_W, B_W), F32),
                            pltpu.VMEM((n_par, C_QP, C_VW), F32)]),
        out_shape=[jax.ShapeDtypeStruct((bsz, seq, D_MODEL), BF16),
                   jax.ShapeDtypeStruct((bsz, A_HEADS, A_DK, A_DV), F32),
                   jax.ShapeDtypeStruct((bsz, B_HEADS, B_DK, B_DV), F32),
                   jax.ShapeDtypeStruct((bsz, C_HEADS, C_DK, C_DV), F32)],
        compiler_params=pltpu.CompilerParams(dimension_semantics=("parallel", "arbitrary"),
                                             vmem_limit_bytes=VMEM_LIMIT),
        name="mix_prompt",
    )(bound, proj, cos, sin, lbl, hn, w2, gb, gn)


def _head_recurrence(q, k, v, logd, gate, norm_col, s_ref, snew_ref, y_ref, qb_ref, kb_ref, eb_ref):
    n_t = len(q)
    dk, n_seq = q[0].shape
    dv = v[0].shape[0]
    b = [logd[0]]
    for t in range(1, n_t):
        b.append(b[-1] + logd[t])
    o = []
    for t in range(n_t):
        acc = jnp.sum(q[t] * k[t], axis=0, keepdims=True) * v[t]
        for s in range(t):
            sc = jnp.sum(q[t] * k[s] * jnp.exp(b[t] - b[s]), axis=0, keepdims=True)
            acc = acc + sc * v[s]
        o.append(acc)
    rep = lambda x: jnp.broadcast_to(x[:, None, :], (dk, SUBLANES, n_seq))
    for t in range(n_t):
        qb_ref[t] = rep(q[t] * jnp.exp(b[t]))
        kb_ref[t] = rep(k[t] * jnp.exp(b[n_t - 1] - b[t]))
    eb_ref[...] = rep(jnp.exp(b[n_t - 1]))
    chunks = [[] for _ in range(n_t)]
    for vc in range(dv // SUBLANES):
        rows = pl.ds(vc * SUBLANES, SUBLANES)
        v_rows = [v[t][vc * SUBLANES:(vc + 1) * SUBLANES, :] for t in range(n_t)]

        def body(d, carry):
            s_d = s_ref[d, rows, :]
            terms = [eb_ref[d] * s_d] + [kb_ref[t, d] * v_rows[t] for t in range(n_t)]
            while len(terms) > 1:
                terms = [a + c for a, c in zip(terms[::2], terms[1::2])] + terms[len(terms) & ~1:]
            snew_ref[d, rows, :] = terms[0]
            return tuple(carry[t] + qb_ref[t, d] * s_d for t in range(n_t))

        inter = lax.fori_loop(0, dk, body, tuple(jnp.zeros((SUBLANES, n_seq), F32) for _ in range(n_t)),
                              unroll=STATE_ROW_UNROLL)
        for t in range(n_t):
            chunks[t].append(inter[t])
    for t in range(n_t):
        o_t = o[t] + jnp.concatenate(chunks[t], axis=0)
        ms = jnp.mean(o_t * o_t, axis=0, keepdims=True)
        y_t = o_t * lax.rsqrt(ms + EPS) * _silu(gate[t])
        if norm_col is not None:
            y_t = y_t * norm_col
        y_ref[:, t * n_seq:(t + 1) * n_seq] = y_t.astype(BF16)


def _state_slots(rest):
    snew_ref, y_ref, qb_ref, kb_ref, eb_ref = rest[-5:]
    done = snew_ref.shape[0] - 1
    if done:
        snew_ref[0:done] = rest[0][...]
    return snew_ref.at[done], y_ref, qb_ref, kb_ref, eb_ref


def _slabs(x, n_t):
    n_seq = x.shape[1] // n_t
    return [x[:, t * n_seq:(t + 1) * n_seq] for t in range(n_t)]


def _head_rows(ref, dh):
    h = pl.program_id(0)
    return ref[pl.ds(pl.multiple_of(h * dh, 2 * SUBLANES), dh), :]


def _hgrn_head_kernel(q_ref, f_ref, i_ref, g_ref, lbl_ref, hn_ref, s_ref, *rest, layer, n_t):
    snew_ref, y_ref, qb_ref, kb_ref, eb_ref = _state_slots(rest)
    log_f, k_a = _hgrn_gate(f_ref[...], [lbl_ref[i] for i in range(DEPTH)], layer)
    _head_recurrence(_slabs(_silu(q_ref[...]), n_t), _slabs(k_a, n_t), _slabs(i_ref[...], n_t),
                     _slabs(log_f, n_t), _slabs(g_ref[...], n_t), hn_ref[...], s_ref, snew_ref, y_ref,
                     qb_ref, kb_ref, eb_ref)


def _ret_head_kernel(q_ref, k_ref, v_ref, g_ref, cos_ref, sin_ref, lg_ref, s_ref, *rest, n_t):
    snew_ref, y_ref, qb_ref, kb_ref, eb_ref = _state_slots(rest)
    half = B_DK // 2

    def rot(x):
        x1, x2 = x[:half], x[half:]
        c, s = cos_ref[...], sin_ref[...]
        return jnp.concatenate([x1 * c - x2 * s, x1 * s + x2 * c], axis=0)

    q = rot(q_ref[...])
    k = rot(k_ref[...]) * (B_DK ** -0.5)
    logd = jnp.broadcast_to(lg_ref[...], (B_DK, q.shape[1] // n_t))
    _head_recurrence(_slabs(q, n_t), _slabs(k, n_t), _slabs(v_ref[...], n_t), [logd] * n_t,
                     _slabs(g_ref[...], n_t), None, s_ref, snew_ref, y_ref, qb_ref, kb_ref, eb_ref)


def _gla_head_kernel(q_ref, k_ref, v_ref, g_ref, r_ref, w2t_ref, gb_ref, gn_ref, s_ref, *rest, n_t):
    snew_ref, y_ref, qb_ref, kb_ref, eb_ref = _state_slots(rest)
    gk = _dot(_head_rows(w2t_ref, C_DK), r_ref[...].astype(BF16)) + _head_rows(gb_ref, C_DK)
    log_a = jax.nn.log_sigmoid(gk) * (1.0 / C_TAU)
    _head_recurrence(_slabs(_head_rows(q_ref, C_DK) * (C_DK ** -0.5), n_t), _slabs(_head_rows(k_ref, C_DK), n_t),
                     _slabs(_head_rows(v_ref, C_DV), n_t), _slabs(log_a, n_t),
                     _slabs(_head_rows(g_ref, C_DV), n_t), gn_ref[...], s_ref, snew_ref, y_ref,
                     qb_ref, kb_ref, eb_ref)


def _head_call(body, name, projt, segs, extras, extra_specs, state, prev, layer, n_heads, dk, dv, n_t):
    m = projt.shape[1]
    n_seq = m // n_t
    seg_specs = []
    for off, rows, per_head in segs:
        assert off % rows == 0
        if per_head:
            seg_specs.append(pl.BlockSpec((rows, m), functools.partial(lambda i, h: (i + h, 0), off // rows)))
        else:
            seg_specs.append(pl.BlockSpec((rows, m), functools.partial(lambda i, h: (i, 0), off // rows)))
    layers = lambda n: pl.BlockSpec((n, None, dk, dv, n_seq), lambda h: (0, h, 0, 0, 0))
    in_specs = seg_specs + extra_specs + [pl.BlockSpec((None, None, dk, dv, n_seq), lambda h: (layer, h, 0, 0, 0))]
    operands = [projt] * len(segs) + extras + [state]
    if prev is not None:
        in_specs.append(layers(layer))
        operands.append(prev)
    rep_rows = pltpu.VMEM((n_t, dk, SUBLANES, n_seq), F32)
    return pl.pallas_call(
        body,
        grid=(n_heads,),
        in_specs=in_specs,
        out_specs=[layers(layer + 1), pl.BlockSpec((dv, m), lambda h: (h, 0))],
        out_shape=[jax.ShapeDtypeStruct((layer + 1,) + state.shape[1:], F32),
                   jax.ShapeDtypeStruct((n_heads * dv, m), BF16)],
        scratch_shapes=[rep_rows, rep_rows, pltpu.VMEM((dk, SUBLANES, n_seq), F32)],
        compiler_params=pltpu.CompilerParams(dimension_semantics=("parallel",), vmem_limit_bytes=VMEM_LIMIT),
        name=name,
    )(*operands)


def _rope_angles(pos):
    half = B_DK // 2
    inv_freq = ROPE_BASE ** (-jnp.arange(half, dtype=F32) / half)
    return pos[:, None] * inv_freq[None, :]


def _rope_tables(pos):
    ang = _rope_angles(pos)
    cos, sin = jnp.cos(ang), jnp.sin(ang)
    cos_h = jnp.concatenate([cos, cos], axis=1)
    sin_h = jnp.concatenate([-sin, sin], axis=1)
    return jnp.tile(cos_h, (1, B_HEADS)), jnp.tile(sin_h, (1, B_HEADS))


def _rope_tables_channel_major(pos, n_seq):
    ang = _rope_angles(pos)
    rep = lambda a: jnp.repeat(a.T, n_seq, axis=1)
    return rep(jnp.cos(ang)), rep(jnp.sin(ang))


def _pad_w_in(w):
    z = lambda n: jnp.zeros(w.shape[:2] + (n,), w.dtype)
    c0 = 2560
    return jnp.concatenate([
        w[..., :c0],
        w[..., c0:c0 + C_QW], z(C_QP - C_QW),
        w[..., c0 + C_QW:c0 + 2 * C_QW], z(C_QP - C_QW),
        w[..., c0 + 2 * C_QW:c0 + 2 * C_QW + 2 * C_VW],
        w[..., c0 + 2 * C_QW + 2 * C_VW:], z(C_RP - C_RANK)], axis=-1)


def kernel(x_prompt, x_sample, state_hgrn, state_ret, state_gla, ffn1_norm, ffn1_w_in, ffn1_w_out, mix_norm, w_in, hgrn_lb_logits, hgrn_norm, gla_w2, gla_b, gla_norm, w_out, ffn2_norm, ffn2_w_in, ffn2_w_out, final_norm):
    bsz, seq, _ = x_prompt.shape
    dbs, dseq, _ = x_sample.shape
    assert seq % ROWS == 0 and seq % TM_IN == 0 and dbs % LANES == 0
    assert all(n % tm == 0 for n in (bsz * seq, dbs * dseq) for tm in (TM_IN, TM_OUT))

    f1_in, f1_out = ffn1_w_in.astype(BF16), ffn1_w_out.astype(BF16)
    f2_in, f2_out = ffn2_w_in.astype(BF16), ffn2_w_out.astype(BF16)
    wo = w_out.astype(BF16)
    wmix = _pad_w_in(w_in).astype(BF16)
    stacked = lambda a: a.reshape(DEPTH, 1, -1).astype(F32)
    n1, nm, n2 = stacked(ffn1_norm), stacked(mix_norm), stacked(ffn2_norm)
    nf = final_norm.reshape(1, -1).astype(F32)
    lbl = hgrn_lb_logits.astype(F32)
    w2_pad = jnp.zeros((DEPTH, C_RP, C_QP), BF16).at[:, :C_RANK, :C_QW].set(gla_w2.astype(BF16))
    gb_pad = jnp.zeros((DEPTH, 1, C_QP), F32).at[:, 0, :C_QW].set(gla_b)

    cos_p, sin_p = _rope_tables(jnp.arange(seq, dtype=F32))
    x = x_prompt.reshape(bsz * seq, D_MODEL)
    p_states = []
    for l in range(DEPTH):
        x1, proj, bound = _ffn_proj(x, n1, f1_in, f1_out, nm, wmix, l, (w2_pad, gb_pad))
        y, sa, sb, sc = _mix_prompt(bound[:, 0, 0], proj.reshape(bsz, seq, NP), cos_p, sin_p, lbl,
                                    jnp.tile(hgrn_norm[l], A_HEADS).reshape(1, A_W), w2_pad[l], gb_pad[l],
                                    jnp.tile(gla_norm[l], C_HEADS).reshape(1, C_VW), l)
        x = _out_ffn(x1, [y.reshape(bsz * seq, D_MODEL)], wo, n2, f2_in, f2_out, nf, l, False)
        p_states.append((sa, sb, sc))
    y_prompt = x.reshape(bsz, seq, D_MODEL)

    m = dbs * dseq
    x = jnp.transpose(x_sample, (1, 0, 2)).reshape(m, D_MODEL)
    states = [jnp.transpose(s, (0, 2, 3, 4, 1)) for s in (state_hgrn, state_ret, state_gla)]
    new = [None, None, None]
    wmix_t = jnp.swapaxes(wmix, 1, 2)
    cos_s, sin_s = _rope_tables_channel_major(PAST_LEN + jnp.arange(dseq, dtype=F32), dbs)
    lg = jnp.asarray(_LOG_GAMMA, F32).reshape(B_HEADS, 1, 1)
    lbl_col = lbl.reshape(DEPTH, A_W, 1)
    w2_t = jnp.swapaxes(w2_pad, 1, 2)
    gb_col = jnp.swapaxes(gb_pad, 1, 2)
    whole = lambda shape: pl.BlockSpec(shape, lambda h: (0,) * len(shape))
    per_head = lambda off, rows: (off, rows, True)
    segment = lambda off, rows: (off, rows, False)
    for l in range(DEPTH):
        x1, projt = _ffn_proj(x, n1, f1_in, f1_out, nm, wmix_t, l)
        new[0], ya = _head_call(
            functools.partial(_hgrn_head_kernel, layer=l, n_t=dseq), "mix_sample_hgrn", projt,
            [per_head(O_AQ, A_DK), per_head(O_AF, A_DK), per_head(O_AI, A_DV), per_head(O_AG, A_DV)],
            [lbl_col, hgrn_norm[l].reshape(A_DV, 1)],
            [pl.BlockSpec((DEPTH, A_DK, 1), lambda h: (0, h, 0)), whole((A_DV, 1))],
            states[0], new[0], l, A_HEADS, A_DK, A_DV, dseq)
        new[1], yb = _head_call(
            functools.partial(_ret_head_kernel, n_t=dseq), "mix_sample_ret", projt,
            [per_head(O_BQ, B_DK), per_head(O_BK, B_DK), per_head(O_BV, B_DV), per_head(O_BG, B_DV)],
            [cos_s, sin_s, lg],
            [whole((B_DK // 2, m)), whole((B_DK // 2, m)), pl.BlockSpec((None, 1, 1), lambda h: (h, 0, 0))],
            states[1], new[1], l, B_HEADS, B_DK, B_DV, dseq)
        new[2], yc = _head_call(
            functools.partial(_gla_head_kernel, n_t=dseq), "mix_sample_gla", projt,
            [segment(O_CQ, C_QP), segment(O_CK, C_QP), segment(O_CV, C_VW), segment(O_CG, C_VW),
             segment(O_CR, C_RP)],
            [w2_t[l], gb_col[l], gla_norm[l].reshape(C_DV, 1)],
            [whole((C_QP, C_RP)), whole((C_QP, 1)), whole((C_DV, 1))],
            states[2], new[2], l, C_HEADS, C_DK, C_DV, dseq)
        x = _out_ffn(x1, [ya, yb, yc], wo, n2, f2_in, f2_out, nf, l, True)
    y_sample = jnp.transpose(x.reshape(dseq, dbs, D_MODEL), (1, 0, 2))
    s_states = [jnp.transpose(s, (0, 4, 1, 2, 3)) for s in new]

    stack = lambda i: jnp.stack([s[i] for s in p_states])
    return (y_prompt, y_sample, stack(0), stack(1), stack(2), s_states[0], s_states[1], s_states[2])
```

```python
import functools
import math

import jax
import jax.numpy as jnp
from jax import lax
from jax.experimental import pallas as pl
from jax.experimental.pallas import tpu as pltpu

F32, BF16 = jnp.float32, jnp.bfloat16

D_MODEL = 1024
DEPTH = 2
A_HEADS, A_DK, A_DV = 4, 64, 64
B_HEADS, B_DK, B_DV = 6, 64, 64
C_HEADS, C_DK, C_DV = 4, 48, 96
C_RANK = 16
C_TAU = 16.0
D_FF = 2816
ROPE_BASE = 10000.0
EPS = 1e-6
PAST_LEN = 16384

A_W = A_HEADS * A_DK
B_W = B_HEADS * B_DK
C_QW = C_HEADS * C_DK
C_VW = C_HEADS * C_DV
LANES = 128
SUBLANES = 8
MXU_DIM = 256
C_QP = 256
C_RP = LANES

O_AQ, O_AF, O_AI, O_AG = 0, 256, 512, 768
O_BQ, O_BK, O_BV, O_BG = 1024, 1408, 1792, 2176
O_CQ, O_CK, O_CV, O_CG, O_CR = 2560, 2816, 3072, 3456, 3840
NP = O_CR + C_RP

ROWS = 128
SEQS_PER_STEP = 4
FACTOR_BLOCK = 64
FACTOR_MAX_EXP = 80.0
STATE_ROW_UNROLL = 4
FF_CHUNK = 256
TM_IN = 512
TM_OUT = 1024
VMEM_CAPACITY = 64 * 1024 * 1024
VMEM_LIMIT = VMEM_CAPACITY - 2 * 1024 * 1024


def _dot(a, b):
    return jnp.dot(a, b, preferred_element_type=F32)


def _dot_nt(a, b):
    return lax.dot_general(a, b, (((1,), (1,)), ((), ())), preferred_element_type=F32)


def _dot_tn(a, b):
    return lax.dot_general(a, b, (((0,), (0,)), ((), ())), preferred_element_type=F32)


def _iota(shape, dim):
    return lax.broadcasted_iota(jnp.int32, shape, dim)


def _split3(x):
    hi = x.astype(BF16)
    r = x - hi.astype(F32)
    mid = r.astype(BF16)
    lo = (r - mid.astype(F32)).astype(BF16)
    return hi, mid, lo


def _dot01(mat01, x):
    hi, mid, lo = _split3(x)
    return _dot(mat01, hi) + _dot(mat01, mid) + _dot(mat01, lo)


def _rms(x, w):
    ms = jnp.mean(x * x, axis=-1, keepdims=True)
    return x * lax.rsqrt(ms + EPS) * w


def _silu(x):
    return x * jax.nn.sigmoid(x)


def _swiglu(h_bf, win_ref, wout_ref, act_ref):
    for c in range(D_FF // FF_CHUNK):
        lo = c * FF_CHUNK
        g = _dot(h_bf, win_ref[:, lo:lo + FF_CHUNK])
        u = _dot(h_bf, win_ref[:, D_FF + lo:D_FF + lo + FF_CHUNK])
        act_ref[:, lo:lo + FF_CHUNK] = (_silu(g) * u).astype(BF16)
    return _dot(act_ref[...], wout_ref[...])


def _lower_bound(logit_rows, layer):
    mx = functools.reduce(jnp.maximum, logit_rows)
    ex = [jnp.exp(r - mx) for r in logit_rows]
    tot = functools.reduce(lambda a, c: a + c, ex)
    acc = functools.reduce(lambda a, c: a + c, ex[:layer + 1]) / tot
    return acc - ex[0] / tot


def _hgrn_gate(af, logit_rows, layer):
    if layer == 0:
        return jax.nn.log_sigmoid(af), jax.nn.sigmoid(-af)
    lb = _lower_bound(logit_rows, layer)
    log_f = jnp.logaddexp(jnp.log(lb), jnp.log1p(-lb) + jax.nn.log_sigmoid(af))
    return log_f, (1.0 - lb) * jax.nn.sigmoid(-af)


def _gla_log_decay(low_rank, w2_ref, gb_ref):
    return jax.nn.log_sigmoid(_dot(low_rank.astype(BF16), w2_ref[...]) + gb_ref[...]) * (1.0 / C_TAU)


def _decay_bound(abs_log_decay):
    rows, c = abs_log_decay.shape
    half = FACTOR_BLOCK // 2
    return jnp.max(jnp.sum(abs_log_decay.reshape(rows // half, half, c), axis=1))


def _ffn_proj_kernel(x_ref, n1_ref, win_ref, wout_ref, nm_ref, wmix_ref, *refs, channel_major):
    if channel_major:
        x1_ref, proj_ref, act_ref = refs
    else:
        w2_ref, gb_ref, x1_ref, proj_ref, bound_ref, act_ref = refs
    x = x_ref[...]
    h = _rms(x, n1_ref[...]).astype(BF16)
    x1 = x + 0.5 * _swiglu(h, win_ref, wout_ref, act_ref)
    x1_ref[...] = x1
    hm = _rms(x1, nm_ref[...]).astype(BF16)
    if channel_major:
        proj_ref[...] = _dot_nt(wmix_ref[...], hm)
        return
    proj = _dot(hm, wmix_ref[...])
    proj_ref[...] = proj
    soft = lambda z: jnp.maximum(-z, 0.0) + math.log(2.0)
    gk = _dot(proj[:, O_CR:O_CR + C_RP].astype(BF16), w2_ref[...]) + gb_ref[...]
    bound = jnp.maximum(_decay_bound(soft(proj[:, O_AF:O_AF + A_W])), _decay_bound(soft(gk) * (1.0 / C_TAU)))
    bound_ref[...] = jnp.full(bound_ref.shape, bound, F32)


def _out_ffn_kernel(x_ref, *refs, final, channel_major):
    n_y = 3 if channel_major else 1
    y_refs = refs[:n_y]
    wo_ref, n2_ref, win_ref, wout_ref, nf_ref, o_ref, act_ref = refs[n_y:]
    x2 = x_ref[...]
    if channel_major:
        lo = 0
        for y_ref in y_refs:
            w = y_ref.shape[0]
            x2 = x2 + _dot_tn(y_ref[...], wo_ref[lo:lo + w, :])
            lo += w
    else:
        x2 = x2 + _dot(y_refs[0][...], wo_ref[...])
    h = _rms(x2, n2_ref[...]).astype(BF16)
    x3 = x2 + 0.5 * _swiglu(h, win_ref, wout_ref, act_ref)
    if final:
        x3 = _rms(x3, nf_ref[...])
    o_ref[...] = x3


def _resident(shape, layer=None):
    if layer is None:
        return pl.BlockSpec(shape, lambda i: (0,) * len(shape), pipeline_mode=pl.Buffered(1))
    return pl.BlockSpec((None,) + shape, lambda i: (layer,) + (0,) * len(shape), pipeline_mode=pl.Buffered(1))


def _row_tile(tm, width):
    return pl.BlockSpec((tm, width), lambda i: (i, 0))


def _col_tile(tm, height):
    return pl.BlockSpec((height, tm), lambda i: (0, i))


def _ffn_proj(x, n1, win, wout, nm, wmix, layer, gate_params=None):
    m = x.shape[0]
    tm = TM_IN
    channel_major = gate_params is None
    in_specs = [_row_tile(tm, D_MODEL), _resident((1, D_MODEL), layer), _resident((D_MODEL, 2 * D_FF), layer),
                _resident((D_FF, D_MODEL), layer), _resident((1, D_MODEL), layer)]
    if channel_major:
        in_specs += [_resident((NP, D_MODEL), layer)]
        out_specs = [_row_tile(tm, D_MODEL), _col_tile(tm, NP)]
        out_shape = [jax.ShapeDtypeStruct((m, D_MODEL), F32), jax.ShapeDtypeStruct((NP, m), F32)]
        gate_params = ()
    else:
        in_specs += [_resident((D_MODEL, NP), layer), _resident((C_RP, C_QP), layer), _resident((1, C_QP), layer)]
        out_specs = [_row_tile(tm, D_MODEL), _row_tile(tm, NP),
                     pl.BlockSpec((1, SUBLANES, LANES), lambda i: (i, 0, 0))]
        out_shape = [jax.ShapeDtypeStruct((m, D_MODEL), F32), jax.ShapeDtypeStruct((m, NP), F32),
                     jax.ShapeDtypeStruct((m // tm, SUBLANES, LANES), F32)]
    return pl.pallas_call(
        functools.partial(_ffn_proj_kernel, channel_major=channel_major),
        grid=(m // tm,),
        in_specs=in_specs,
        out_specs=out_specs,
        out_shape=out_shape,
        scratch_shapes=[pltpu.VMEM((tm, D_FF), BF16)],
        compiler_params=pltpu.CompilerParams(dimension_semantics=("parallel",), vmem_limit_bytes=VMEM_LIMIT),
        name="ffn1_proj",
    )(x, n1, win, wout, nm, wmix, *gate_params)


def _out_ffn(x, ys, wo, n2, win, wout, nf, layer, channel_major):
    m = x.shape[0]
    tm = TM_OUT
    y_specs = [_col_tile(tm, y.shape[0]) for y in ys] if channel_major else [_row_tile(tm, D_MODEL)]
    return pl.pallas_call(
        functools.partial(_out_ffn_kernel, final=layer == DEPTH - 1, channel_major=channel_major),
        grid=(m // tm,),
        in_specs=[_row_tile(tm, D_MODEL)] + y_specs + [
            _resident((D_MODEL, D_MODEL), layer), _resident((1, D_MODEL), layer),
            _resident((D_MODEL, 2 * D_FF), layer), _resident((D_FF, D_MODEL), layer), _resident((1, D_MODEL))],
        out_specs=_row_tile(tm, D_MODEL),
        out_shape=jax.ShapeDtypeStruct((m, D_MODEL), F32),
        scratch_shapes=[pltpu.VMEM((tm, D_FF), BF16)],
        compiler_params=pltpu.CompilerParams(dimension_semantics=("parallel",), vmem_limit_bytes=VMEM_LIMIT),
        name="outproj_ffn2",
    )(x, *ys, wo, n2, win, wout, nf)


def _log2(n):
    assert n & (n - 1) == 0
    return n.bit_length() - 1


def _head_mask(width, dh, h):
    lane = _iota((1, width), 1)
    return (lane >= dh * h) & (lane < dh * (h + 1))


def _head_grid_mask(rows, cols, dr, dc, n_heads):
    r = _iota((rows, cols), 0)
    c = _iota((rows, cols), 1)
    m = None
    for h in range(n_heads):
        mh = (r >= dr * h) & (r < dr * (h + 1)) & (c >= dc * h) & (c < dc * (h + 1))
        m = mh if m is None else (m | mh)
    return m


def _seg_mean_sq(o, dh, n_heads):
    w = o.shape[1]
    if w > MXU_DIM and MXU_DIM % dh == 0:
        parts = [_seg_mean_sq(o[:, lo:min(lo + MXU_DIM, w)], dh, (min(lo + MXU_DIM, w) - lo) // dh)
                 for lo in range(0, w, MXU_DIM)]
        return jnp.concatenate(parts, axis=1)
    ones = jnp.where(_head_grid_mask(w, w, dh, dh, n_heads), 1.0, 0.0).astype(BF16)
    return _dot((o * o).astype(BF16), ones) * (1.0 / dh)


def _head_norm(o, dh, n_heads):
    return o * lax.rsqrt(_seg_mean_sq(o, dh, n_heads) + EPS)


def _tile_cumsum(g):
    rows = _iota((ROWS, ROWS), 0)
    cols = _iota((ROWS, ROWS), 1)
    return _dot01(jnp.where(cols <= rows, 1.0, 0.0).astype(BF16), g)


def _block_row(b, size, idx):
    c = b.shape[1]
    if size >= SUBLANES:
        b3 = b.reshape(ROWS // size, size, c)
        return jnp.broadcast_to(b3[:, idx:idx + 1, :], b3.shape).reshape(ROWS, c)
    b8 = b.reshape(ROWS // SUBLANES, SUBLANES, c)
    sub = _iota(b8.shape, 1)
    ref = b8[:, idx:idx + 1, :]
    for blk in range(1, SUBLANES // size):
        lo = blk * size
        ref = jnp.where(sub >= lo, b8[:, lo + idx:lo + idx + 1, :], ref)
    return jnp.broadcast_to(ref, b8.shape).reshape(ROWS, c)


def _stack_heads(x, dh, n_heads, extra_mask=None):
    parts = []
    for h in range(n_heads):
        m = _head_mask(x.shape[1], dh, h)
        if extra_mask is not None:
            m = m & extra_mask
        parts.append(jnp.where(m, x, 0.0))
    return jnp.concatenate(parts, axis=0).astype(BF16)


def _apply_scores(p, v, dv, n_heads):
    pcat = jnp.concatenate([p[h].astype(BF16) for h in range(n_heads)], axis=1)
    return _dot(pcat, _stack_heads(v, dv, n_heads))


def _tree_levels(q, k, b, dk, n_heads, first):
    rows = _iota((ROWS, ROWS), 0)
    cols = _iota((ROWS, ROWS), 1)
    rowc = _iota((ROWS, q.shape[1]), 0)
    total = jnp.zeros((n_heads, ROWS, ROWS), F32)
    m = first
    while m < ROWS:
        ref = _block_row(b, 2 * m, m - 1)
        up = (rowc & (2 * m - 1)) >= m
        e = jnp.exp(jnp.where(up, b - ref, ref - b))
        ql = _stack_heads(q * e, dk, n_heads, extra_mask=up)
        kl = jnp.where(up, 0.0, k * e).astype(BF16)
        sc = _dot_nt(ql, kl).reshape(n_heads, ROWS, ROWS)
        sh = _log2(2 * m)
        total = total + jnp.where(((rows >> sh) == (cols >> sh))[None], sc, 0.0)
        m *= 2
    return total


def _block_deviation(b):
    return b - _block_row(b, FACTOR_BLOCK, FACTOR_BLOCK // 2 - 1)


def _gated_intra(q, k, v, b, dev, dk, dv, n_heads, factorise):
    rows = _iota((ROWS, ROWS), 0)
    cols = _iota((ROWS, ROWS), 1)
    if factorise:
        qf = _stack_heads(q * jnp.exp(dev), dk, n_heads)
        kf = (k * jnp.exp(-dev)).astype(BF16)
        sc = _dot_nt(qf, kf).reshape(n_heads, ROWS, ROWS)
        sh = _log2(FACTOR_BLOCK)
        keep = ((rows >> sh) == (cols >> sh)) & (cols <= rows)
        total = jnp.where(keep[None], sc, 0.0) + _tree_levels(q, k, b, dk, n_heads, FACTOR_BLOCK)
    else:
        sc = _dot_nt(_stack_heads(q, dk, n_heads), k.astype(BF16)).reshape(n_heads, ROWS, ROWS)
        total = jnp.where((rows == cols)[None], sc, 0.0) + _tree_levels(q, k, b, dk, n_heads, 1)
    return _apply_scores(total, v, dv, n_heads)


def _column(row_vec):
    c = row_vec.shape[1]
    return jnp.transpose(jnp.broadcast_to(row_vec, (SUBLANES, c)))[:, 0:1]


def _gated_tile(q, k, v, b, dev, dk, dv, n_heads, factorise, s_ref):
    o = _gated_intra(q, k, v, b, dev, dk, dv, n_heads, factorise)
    s0 = s_ref[...]
    o = o + _dot((q * jnp.exp(b)).astype(BF16), s0.astype(BF16))
    b_last = b[ROWS - 1:ROWS, :]
    khat = (k * jnp.exp(b_last - b)).astype(BF16)
    ds = _dot_tn(khat, v.astype(BF16))
    keep = _head_grid_mask(q.shape[1], v.shape[1], dk, dv, n_heads)
    s_ref[...] = s0 * _column(jnp.exp(b_last)) + jnp.where(keep, ds, 0.0)
    return o


_LOG_GAMMA = tuple(math.log(1.0 - 2.0 ** (-5.0 - h)) for h in range(B_HEADS))


def _per_head_lanes(width, dh, values):
    lane = _iota((1, width), 1)
    out = jnp.zeros((1, width), F32)
    for h, val in enumerate(values):
        out = jnp.where((lane >= dh * h) & (lane < dh * (h + 1)), val, out)
    return out


def _rotary(x, cos, sin_signed):
    w = x.shape[1]
    lane = _iota((1, w), 1)
    first_half = (lane & (B_DK - 1)) < (B_DK // 2)
    partner = jnp.where(first_half, pltpu.roll(x, w - B_DK // 2, 1), pltpu.roll(x, B_DK // 2, 1))
    return x * cos + partner * sin_signed


def _lane_groups(width):
    return [(lo, min(lo + MXU_DIM, width)) for lo in range(0, width, MXU_DIM)]


def _retention_tile(q, k, v, s_ref):
    rows = _iota((ROWS, ROWS), 0)
    cols = _iota((ROWS, ROWS), 1)
    causal = cols <= rows
    dist = (rows - cols).astype(F32)
    tau = _iota((ROWS, 1), 0).astype(F32)
    lg = _per_head_lanes(B_W, B_DK, _LOG_GAMMA)
    qhat = (q * jnp.exp(lg * (tau + 1.0))).astype(BF16)
    khat = (k * jnp.exp(lg * (ROWS - 1.0 - tau))).astype(BF16)
    decay = _column(_per_head_lanes(B_W, B_DK, [math.exp(ROWS * g) for g in _LOG_GAMMA]))
    v_bf = v.astype(BF16)
    outs = []
    for lo, hi in _lane_groups(B_W):
        nh = (hi - lo) // B_DK
        sc = _dot_nt(_stack_heads(q[:, lo:hi], B_DK, nh), k[:, lo:hi].astype(BF16)).reshape(nh, ROWS, ROWS)
        gam = jnp.stack([jnp.where(causal, jnp.exp(dist * g), 0.0) for g in _LOG_GAMMA[lo // B_DK:hi // B_DK]])
        s0 = s_ref[lo:hi, lo:hi]
        outs.append(_apply_scores(sc * gam, v[:, lo:hi], B_DV, nh) + _dot(qhat[:, lo:hi], s0.astype(BF16)))
        ds = _dot_tn(khat[:, lo:hi], v_bf[:, lo:hi])
        keep = _head_grid_mask(hi - lo, hi - lo, B_DK, B_DV, nh)
        s_ref[lo:hi, lo:hi] = s0 * decay[lo:hi, :] + jnp.where(keep, ds, 0.0)
    return jnp.concatenate(outs, axis=1)


def _mix_tile(proj_ref, cos_ref, sin_ref, lbl_ref, hn_ref, w2_ref, gb_ref, gn_ref, sa_ref, sb_ref, sc_ref,
              layer, factorise):
    seg = lambda off, w: proj_ref[:, off:off + w]
    log_f, k_a = _hgrn_gate(seg(O_AF, A_W), [lbl_ref[i:i + 1, :] for i in range(DEPTH)], layer)
    b_a = _tile_cumsum(log_f)
    b_c = _tile_cumsum(_gla_log_decay(seg(O_CR, C_RP), w2_ref, gb_ref))
    dev_a = _block_deviation(b_a) if factorise else None
    dev_c = _block_deviation(b_c) if factorise else None
    o_a = _gated_tile(_silu(seg(O_AQ, A_W)), k_a, seg(O_AI, A_W), b_a, dev_a, A_DK, A_DV, A_HEADS, factorise,
                      sa_ref)
    y_a = _head_norm(o_a, A_DV, A_HEADS) * hn_ref[...] * _silu(seg(O_AG, A_W))
    q_b = _rotary(seg(O_BQ, B_W), cos_ref[...], sin_ref[...])
    k_b = _rotary(seg(O_BK, B_W), cos_ref[...], sin_ref[...]) * (B_DK ** -0.5)
    o_b = _retention_tile(q_b, k_b, seg(O_BV, B_W), sb_ref)
    y_b = _head_norm(o_b, B_DV, B_HEADS) * _silu(seg(O_BG, B_W))
    o_c = _gated_tile(seg(O_CQ, C_QP) * (C_DK ** -0.5), seg(O_CK, C_QP), seg(O_CV, C_VW), b_c, dev_c,
                      C_DK, C_DV, C_HEADS, factorise, sc_ref)
    y_c = _head_norm(o_c, C_DV, C_HEADS) * gn_ref[...] * _silu(seg(O_CG, C_VW))
    return jnp.concatenate([y_a, y_b, y_c], axis=1).astype(BF16)


def _mix_prompt_kernel(bound_ref, proj_ref, cos_ref, sin_ref, lbl_ref, hn_ref, w2_ref, gb_ref, gn_ref,
                       y_ref, sa_ref, sb_ref, sc_ref, sa_acc, sb_acc, sc_acc, *, layer, bounds_per_seq,
                       tiles_per_bound):
    t = pl.program_id(1)

    @pl.when(t == 0)
    def _():
        sa_acc[...] = jnp.zeros_like(sa_acc)
        sb_acc[...] = jnp.zeros_like(sb_acc)
        sc_acc[...] = jnp.zeros_like(sc_acc)

    n_par = proj_ref.shape[0]

    def body(factorise):
        for i in range(n_par):
            y_ref[i] = _mix_tile(proj_ref.at[i], cos_ref, sin_ref, lbl_ref, hn_ref, w2_ref, gb_ref, gn_ref,
                                 sa_acc.at[i], sb_acc.at[i], sc_acc.at[i], layer, factorise)

    safe = None
    for i in range(n_par):
        ok = bound_ref[(pl.program_id(0) * n_par + i) * bounds_per_seq + t // tiles_per_bound] < FACTOR_MAX_EXP
        safe = ok if safe is None else jnp.logical_and(safe, ok)

    @pl.when(safe)
    def _():
        body(True)

    @pl.when(jnp.logical_not(safe))
    def _():
        body(False)

    @pl.when(t == pl.num_programs(1) - 1)
    def _():
        for i in range(n_par):
            for h in range(A_HEADS):
                sa_ref[i, h] = sa_acc[i, A_DK * h:A_DK * (h + 1), A_DV * h:A_DV * (h + 1)]
            for h in range(B_HEADS):
                sb_ref[i, h] = sb_acc[i, B_DK * h:B_DK * (h + 1), B_DV * h:B_DV * (h + 1)]
            for h in range(C_HEADS):
                sc_ref[i, h] = sc_acc[i, C_DK * h:C_DK * (h + 1), C_DV * h:C_DV * (h + 1)]


def _const(shape):
    return pl.BlockSpec(shape, lambda *_: (0,) * len(shape))


def _mix_prompt(bound, proj, cos, sin, lbl, hn, w2, gb, gn, layer):
    bsz, seq, _ = proj.shape
    n_par = SEQS_PER_STEP if bsz % SEQS_PER_STEP == 0 else 1
    bounds_per_seq = bound.shape[0] // bsz
    assert (seq // ROWS) % bounds_per_seq == 0
    tile = lambda w: pl.BlockSpec((n_par, ROWS, w), lambda b, t, _: (b, t, 0))
    tab = pl.BlockSpec((ROWS, B_W), lambda b, t, _: (t, 0))
    st = lambda h, dk, dv: pl.BlockSpec((n_par, h, dk, dv), lambda b, t, _: (b, 0, 0, 0))
    return pl.pallas_call(
        functools.partial(_mix_prompt_kernel, layer=layer, bounds_per_seq=bounds_per_seq,
                          tiles_per_bound=(seq // ROWS) // bounds_per_seq),
        grid_spec=pltpu.PrefetchScalarGridSpec(
            num_scalar_prefetch=1,
            grid=(bsz // n_par, seq // ROWS),
            in_specs=[tile(NP), tab, tab, _const((DEPTH, A_W)), _const((1, A_W)), _const((C_RP, C_QP)),
                      _const((1, C_QP)), _const((1, C_VW))],
            out_specs=[tile(D_MODEL), st(A_HEADS, A_DK, A_DV), st(B_HEADS, B_DK, B_DV),
                       st(C_HEADS, C_DK, C_DV)],
            scratch_shapes=[pltpu.VMEM((n_par, A_W, A_W), F32), pltpu.VMEM((n_par, B_W, B_W), F32),
                            pltpu.VMEM((n_par, C_QP, C_VW), F32)]),
        out_shape=[jax.ShapeDtypeStruct((bsz, seq, D_MODEL), BF16),
                   jax.ShapeDtypeStruct((bsz, A_HEADS, A_DK, A_DV), F32),
                   jax.ShapeDtypeStruct((bsz, B_HEADS, B_DK, B_DV), F32),
                   jax.ShapeDtypeStruct((bsz, C_HEADS, C_DK, C_DV), F32)],
        compiler_params=pltpu.CompilerParams(dimension_semantics=("parallel", "arbitrary"),
                                             vmem_limit_bytes=VMEM_LIMIT),
        name="mix_prompt",
    )(bound, proj, cos, sin, lbl, hn, w2, gb, gn)


def _head_recurrence(q, k, v, logd, gate, norm_col, s_ref, snew_ref, y_ref, qb_ref, kb_ref, eb_ref):
    n_t = len(q)
    dk, n_seq = q[0].shape
    dv = v[0].shape[0]
    b = [logd[0]]
    for t in range(1, n_t):
        b.append(b[-1] + logd[t])
    o = []
    for t in range(n_t):
        acc = jnp.sum(q[t] * k[t], axis=0, keepdims=True) * v[t]
        for s in range(t):
            sc = jnp.sum(q[t] * k[s] * jnp.exp(b[t] - b[s]), axis=0, keepdims=True)
            acc = acc + sc * v[s]
        o.append(acc)
    rep = lambda x: jnp.broadcast_to(x[:, None, :], (dk, SUBLANES, n_seq))
    for t in range(n_t):
        qb_ref[t] = rep(q[t] * jnp.exp(b[t]))
        kb_ref[t] = rep(k[t] * jnp.exp(b[n_t - 1] - b[t]))
    eb_ref[...] = rep(jnp.exp(b[n_t - 1]))
    chunks = [[] for _ in range(n_t)]
    for vc in range(dv // SUBLANES):
        rows = pl.ds(vc * SUBLANES, SUBLANES)
        v_rows = [v[t][vc * SUBLANES:(vc + 1) * SUBLANES, :] for t in range(n_t)]

        def body(d, carry):
            s_d = s_ref[d, rows, :]
            terms = [eb_ref[d] * s_d] + [kb_ref[t, d] * v_rows[t] for t in range(n_t)]
            while len(terms) > 1:
                terms = [a + c for a, c in zip(terms[::2], terms[1::2])] + terms[len(terms) & ~1:]
            snew_ref[d, rows, :] = terms[0]
            return tuple(carry[t] + qb_ref[t, d] * s_d for t in range(n_t))

        inter = lax.fori_loop(0, dk, body, tuple(jnp.zeros((SUBLANES, n_seq), F32) for _ in range(n_t)),
                              unroll=STATE_ROW_UNROLL)
        for t in range(n_t):
            chunks[t].append(inter[t])
    for t in range(n_t):
        o_t = o[t] + jnp.concatenate(chunks[t], axis=0)
        ms = jnp.mean(o_t * o_t, axis=0, keepdims=True)
        y_t = o_t * lax.rsqrt(ms + EPS) * _silu(gate[t])
        if norm_col is not None:
            y_t = y_t * norm_col
        y_ref[:, t * n_seq:(t + 1) * n_seq] = y_t.astype(BF16)


def _state_slots(rest):
    snew_ref, y_ref, qb_ref, kb_ref, eb_ref = rest[-5:]
    done = snew_ref.shape[0] - 1
    if done:
        snew_ref[0:done] = rest[0][...]
    return snew_ref.at[done], y_ref, qb_ref, kb_ref, eb_ref


def _slabs(x, n_t):
    n_seq = x.shape[1] // n_t
    return [x[:, t * n_seq:(t + 1) * n_seq] for t in range(n_t)]


def _head_rows(ref, dh):
    h = pl.program_id(0)
    return ref[pl.ds(pl.multiple_of(h * dh, 2 * SUBLANES), dh), :]


def _hgrn_head_kernel(q_ref, f_ref, i_ref, g_ref, lbl_ref, hn_ref, s_ref, *rest, layer, n_t):
    snew_ref, y_ref, qb_ref, kb_ref, eb_ref = _state_slots(rest)
    log_f, k_a = _hgrn_gate(f_ref[...], [lbl_ref[i] for i in range(DEPTH)], layer)
    _head_recurrence(_slabs(_silu(q_ref[...]), n_t), _slabs(k_a, n_t), _slabs(i_ref[...], n_t),
                     _slabs(log_f, n_t), _slabs(g_ref[...], n_t), hn_ref[...], s_ref, snew_ref, y_ref,
                     qb_ref, kb_ref, eb_ref)


def _ret_head_kernel(q_ref, k_ref, v_ref, g_ref, cos_ref, sin_ref, lg_ref, s_ref, *rest, n_t):
    snew_ref, y_ref, qb_ref, kb_ref, eb_ref = _state_slots(rest)
    half = B_DK // 2

    def rot(x):
        x1, x2 = x[:half], x[half:]
        c, s = cos_ref[...], sin_ref[...]
        return jnp.concatenate([x1 * c - x2 * s, x1 * s + x2 * c], axis=0)

    q = rot(q_ref[...])
    k = rot(k_ref[...]) * (B_DK ** -0.5)
    logd = jnp.broadcast_to(lg_ref[...], (B_DK, q.shape[1] // n_t))
    _head_recurrence(_slabs(q, n_t), _slabs(k, n_t), _slabs(v_ref[...], n_t), [logd] * n_t,
                     _slabs(g_ref[...], n_t), None, s_ref, snew_ref, y_ref, qb_ref, kb_ref, eb_ref)


def _gla_head_kernel(q_ref, k_ref, v_ref, g_ref, r_ref, w2t_ref, gb_ref, gn_ref, s_ref, *rest, n_t):
    snew_ref, y_ref, qb_ref, kb_ref, eb_ref = _state_slots(rest)
    gk = _dot(_head_rows(w2t_ref, C_DK), r_ref[...].astype(BF16)) + _head_rows(gb_ref, C_DK)
    log_a = jax.nn.log_sigmoid(gk) * (1.0 / C_TAU)
    _head_recurrence(_slabs(_head_rows(q_ref, C_DK) * (C_DK ** -0.5), n_t), _slabs(_head_rows(k_ref, C_DK), n_t),
                     _slabs(_head_rows(v_ref, C_DV), n_t), _slabs(log_a, n_t),
                     _slabs(_head_rows(g_ref, C_DV), n_t), gn_ref[...], s_ref, snew_ref, y_ref,
                     qb_ref, kb_ref, eb_ref)


def _head_call(body, name, projt, segs, extras, extra_specs, state, prev, layer, n_heads, dk, dv, n_t):
    m = projt.shape[1]
    n_seq = m // n_t
    seg_specs = []
    for off, rows, per_head in segs:
        assert off % rows == 0
        if per_head:
            seg_specs.append(pl.BlockSpec((rows, m), functools.partial(lambda i, h: (i + h, 0), off // rows)))
        else:
            seg_specs.append(pl.BlockSpec((rows, m), functools.partial(lambda i, h: (i, 0), off // rows)))
    layers = lambda n: pl.BlockSpec((n, None, dk, dv, n_seq), lambda h: (0, h, 0, 0, 0))
    in_specs = seg_specs + extra_specs + [pl.BlockSpec((None, None, dk, dv, n_seq), lambda h: (layer, h, 0, 0, 0))]
    operands = [projt] * len(segs) + extras + [state]
    if prev is not None:
        in_specs.append(layers(layer))
        operands.append(prev)
    rep_rows = pltpu.VMEM((n_t, dk, SUBLANES, n_seq), F32)
    return pl.pallas_call(
        body,
        grid=(n_heads,),
        in_specs=in_specs,
        out_specs=[layers(layer + 1), pl.BlockSpec((dv, m), lambda h: (h, 0))],
        out_shape=[jax.ShapeDtypeStruct((layer + 1,) + state.shape[1:], F32),
                   jax.ShapeDtypeStruct((n_heads * dv, m), BF16)],
        scratch_shapes=[rep_rows, rep_rows, pltpu.VMEM((dk, SUBLANES, n_seq), F32)],
        compiler_params=pltpu.CompilerParams(dimension_semantics=("parallel",), vmem_limit_bytes=VMEM_LIMIT),
        name=name,
    )(*operands)


def _rope_angles(pos):
    half = B_DK // 2
    inv_freq = ROPE_BASE ** (-jnp.arange(half, dtype=F32) / half)
    return pos[:, None] * inv_freq[None, :]


def _rope_tables(pos):
    ang = _rope_angles(pos)
    cos, sin = jnp.cos(ang), jnp.sin(ang)
    cos_h = jnp.concatenate([cos, cos], axis=1)
    sin_h = jnp.concatenate([-sin, sin], axis=1)
    return jnp.tile(cos_h, (1, B_HEADS)), jnp.tile(sin_h, (1, B_HEADS))


def _rope_tables_channel_major(pos, n_seq):
    ang = _rope_angles(pos)
    rep = lambda a: jnp.repeat(a.T, n_seq, axis=1)
    return rep(jnp.cos(ang)), rep(jnp.sin(ang))


def _pad_w_in(w):
    z = lambda n: jnp.zeros(w.shape[:2] + (n,), w.dtype)
    c0 = 2560
    return jnp.concatenate([
        w[..., :c0],
        w[..., c0:c0 + C_QW], z(C_QP - C_QW),
        w[..., c0 + C_QW:c0 + 2 * C_QW], z(C_QP - C_QW),
        w[..., c0 + 2 * C_QW:c0 + 2 * C_QW + 2 * C_VW],
        w[..., c0 + 2 * C_QW + 2 * C_VW:], z(C_RP - C_RANK)], axis=-1)


def kernel(x_prompt, x_sample, state_hgrn, state_ret, state_gla, ffn1_norm, ffn1_w_in, ffn1_w_out, mix_norm, w_in, hgrn_lb_logits, hgrn_norm, gla_w2, gla_b, gla_norm, w_out, ffn2_norm, ffn2_w_in, ffn2_w_out, final_norm):
    bsz, seq, _ = x_prompt.shape
    dbs, dseq, _ = x_sample.shape
    assert seq % ROWS == 0 and seq % TM_IN == 0 and dbs % LANES == 0
    assert all(n % tm == 0 for n in (bsz * seq, dbs * dseq) for tm in (TM_IN, TM_OUT))

    f1_in, f1_out = ffn1_w_in.astype(BF16), ffn1_w_out.astype(BF16)
    f2_in, f2_out = ffn2_w_in.astype(BF16), ffn2_w_out.astype(BF16)
    wo = w_out.astype(BF16)
    wmix = _pad_w_in(w_in).astype(BF16)
    stacked = lambda a: a.reshape(DEPTH, 1, -1).astype(F32)
    n1, nm, n2 = stacked(ffn1_norm), stacked(mix_norm), stacked(ffn2_norm)
    nf = final_norm.reshape(1, -1).astype(F32)
    lbl = hgrn_lb_logits.astype(F32)
    w2_pad = jnp.zeros((DEPTH, C_RP, C_QP), BF16).at[:, :C_RANK, :C_QW].set(gla_w2.astype(BF16))
    gb_pad = jnp.zeros((DEPTH, 1, C_QP), F32).at[:, 0, :C_QW].set(gla_b)

    cos_p, sin_p = _rope_tables(jnp.arange(seq, dtype=F32))
    x = x_prompt.reshape(bsz * seq, D_MODEL)
    p_states = []
    for l in range(DEPTH):
        x1, proj, bound = _ffn_proj(x, n1, f1_in, f1_out, nm, wmix, l, (w2_pad, gb_pad))
        y, sa, sb, sc = _mix_prompt(bound[:, 0, 0], proj.reshape(bsz, seq, NP), cos_p, sin_p, lbl,
                                    jnp.tile(hgrn_norm[l], A_HEADS).reshape(1, A_W), w2_pad[l], gb_pad[l],
                                    jnp.tile(gla_norm[l], C_HEADS).reshape(1, C_VW), l)
        x = _out_ffn(x1, [y.reshape(bsz * seq, D_MODEL)], wo, n2, f2_in, f2_out, nf, l, False)
        p_states.append((sa, sb, sc))
    y_prompt = x.reshape(bsz, seq, D_MODEL)

    m = dbs * dseq
    x = jnp.transpose(x_sample, (1, 0, 2)).reshape(m, D_MODEL)
    states = [jnp.transpose(s, (0, 2, 3, 4, 1)) for s in (state_hgrn, state_ret, state_gla)]
    new = [None, None, None]
    wmix_t = jnp.swapaxes(wmix, 1, 2)
    cos_s, sin_s = _rope_tables_channel_major(PAST_LEN + jnp.arange(dseq, dtype=F32), dbs)
    lg = jnp.asarray(_LOG_GAMMA, F32).reshape(B_HEADS, 1, 1)
    lbl_col = lbl.reshape(DEPTH, A_W, 1)
    w2_t = jnp.swapaxes(w2_pad, 1, 2)
    gb_col = jnp.swapaxes(gb_pad, 1, 2)
    whole = lambda shape: pl.BlockSpec(shape, lambda h: (0,) * len(shape))
    per_head = lambda off, rows: (off, rows, True)
    segment = lambda off, rows: (off, rows, False)
    for l in range(DEPTH):
        x1, projt = _ffn_proj(x, n1, f1_in, f1_out, nm, wmix_t, l)
        new[0], ya = _head_call(
            functools.partial(_hgrn_head_kernel, layer=l, n_t=dseq), "mix_sample_hgrn", projt,
            [per_head(O_AQ, A_DK), per_head(O_AF, A_DK), per_head(O_AI, A_DV), per_head(O_AG, A_DV)],
            [lbl_col, hgrn_norm[l].reshape(A_DV, 1)],
            [pl.BlockSpec((DEPTH, A_DK, 1), lambda h: (0, h, 0)), whole((A_DV, 1))],
            states[0], new[0], l, A_HEADS, A_DK, A_DV, dseq)
        new[1], yb = _head_call(
            functools.partial(_ret_head_kernel, n_t=dseq), "mix_sample_ret", projt,
            [per_head(O_BQ, B_DK), per_head(O_BK, B_DK), per_head(O_BV, B_DV), per_head(O_BG, B_DV)],
            [cos_s, sin_s, lg],
            [whole((B_DK // 2, m)), whole((B_DK // 2, m)), pl.BlockSpec((None, 1, 1), lambda h: (h, 0, 0))],
            states[1], new[1], l, B_HEADS, B_DK, B_DV, dseq)
        new[2], yc = _head_call(
            functools.partial(_gla_head_kernel, n_t=dseq), "mix_sample_gla", projt,
            [segment(O_CQ, C_QP), segment(O_CK, C_QP), segment(O_CV, C_VW), segment(O_CG, C_VW),
             segment(O_CR, C_RP)],
            [w2_t[l], gb_col[l], gla_norm[l].reshape(C_DV, 1)],
            [whole((C_QP, C_RP)), whole((C_QP, 1)), whole((C_DV, 1))],
            states[2], new[2], l, C_HEADS, C_DK, C_DV, dseq)
        x = _out_ffn(x1, [ya, yb, yc], wo, n2, f2_in, f2_out, nf, l, True)
    y_sample = jnp.transpose(x.reshape(dseq, dbs, D_MODEL), (1, 0, 2))
    s_states = [jnp.transpose(s, (0, 4, 1, 2, 3)) for s in new]

    stack = lambda i: jnp.stack([s[i] for s in p_states])
    return (y_prompt, y_sample, stack(0), stack(1), stack(2), s_states[0], s_states[1], s_states[2])
```

```python
import functools
import math

import jax
import jax.numpy as jnp
from jax import lax
from jax.experimental import pallas as pl
from jax.experimental.pallas import tpu as pltpu

F32, BF16 = jnp.float32, jnp.bfloat16

D_MODEL = 1024
DEPTH = 2
A_HEADS, A_DK, A_DV = 4, 64, 64
B_HEADS, B_DK, B_DV = 6, 64, 64
C_HEADS, C_DK, C_DV = 4, 48, 96
C_RANK = 16
C_TAU = 16.0
D_FF = 2816
ROPE_BASE = 10000.0
EPS = 1e-6
PAST_LEN = 16384

A_W = A_HEADS * A_DK
B_W = B_HEADS * B_DK
C_QW = C_HEADS * C_DK
C_VW = C_HEADS * C_DV
LANES = 128
SUBLANES = 8
MXU_DIM = 256
C_QP = 256
C_RP = LANES

O_AQ, O_AF, O_AI, O_AG = 0, 256, 512, 768
O_BQ, O_BK, O_BV, O_BG = 1024, 1408, 1792, 2176
O_CQ, O_CK, O_CV, O_CG, O_CR = 2560, 2816, 3072, 3456, 3840
NP = O_CR + C_RP

ROWS = 128
SEQS_PER_STEP = 4
FACTOR_BLOCK = 64
FACTOR_MAX_EXP = 80.0
STATE_ROW_UNROLL = 16
FF_CHUNK = 256
TM_IN = 512
TM_OUT = 1024
VMEM_CAPACITY = 64 * 1024 * 1024
VMEM_LIMIT = VMEM_CAPACITY - 2 * 1024 * 1024


def _dot(a, b):
    return jnp.dot(a, b, preferred_element_type=F32)


def _dot_nt(a, b):
    return lax.dot_general(a, b, (((1,), (1,)), ((), ())), preferred_element_type=F32)


def _dot_tn(a, b):
    return lax.dot_general(a, b, (((0,), (0,)), ((), ())), preferred_element_type=F32)


def _iota(shape, dim):
    return lax.broadcasted_iota(jnp.int32, shape, dim)


def _split3(x):
    hi = x.astype(BF16)
    r = x - hi.astype(F32)
    mid = r.astype(BF16)
    lo = (r - mid.astype(F32)).astype(BF16)
    return hi, mid, lo


def _dot01(mat01, x):
    hi, mid, lo = _split3(x)
    return _dot(mat01, hi) + _dot(mat01, mid) + _dot(mat01, lo)


def _rms(x, w):
    ms = jnp.mean(x * x, axis=-1, keepdims=True)
    return x * lax.rsqrt(ms + EPS) * w


def _silu(x):
    return x * jax.nn.sigmoid(x)


def _swiglu(h_bf, win_ref, wout_ref, act_ref):
    for c in range(D_FF // FF_CHUNK):
        lo = c * FF_CHUNK
        g = _dot(h_bf, win_ref[:, lo:lo + FF_CHUNK])
        u = _dot(h_bf, win_ref[:, D_FF + lo:D_FF + lo + FF_CHUNK])
        act_ref[:, lo:lo + FF_CHUNK] = (_silu(g) * u).astype(BF16)
    return _dot(act_ref[...], wout_ref[...])


def _lower_bound(logit_rows, layer):
    mx = functools.reduce(jnp.maximum, logit_rows)
    ex = [jnp.exp(r - mx) for r in logit_rows]
    tot = functools.reduce(lambda a, c: a + c, ex)
    acc = functools.reduce(lambda a, c: a + c, ex[:layer + 1]) / tot
    return acc - ex[0] / tot


def _hgrn_gate(af, logit_rows, layer):
    if layer == 0:
        return jax.nn.log_sigmoid(af), jax.nn.sigmoid(-af)
    lb = _lower_bound(logit_rows, layer)
    log_f = jnp.logaddexp(jnp.log(lb), jnp.log1p(-lb) + jax.nn.log_sigmoid(af))
    return log_f, (1.0 - lb) * jax.nn.sigmoid(-af)


def _gla_log_decay(low_rank, w2_ref, gb_ref):
    return jax.nn.log_sigmoid(_dot(low_rank.astype(BF16), w2_ref[...]) + gb_ref[...]) * (1.0 / C_TAU)


def _decay_bound(abs_log_decay):
    rows, c = abs_log_decay.shape
    half = FACTOR_BLOCK // 2
    return jnp.max(jnp.sum(abs_log_decay.reshape(rows // half, half, c), axis=1))


def _ffn_proj_kernel(x_ref, n1_ref, win_ref, wout_ref, nm_ref, wmix_ref, *refs, channel_major):
    if channel_major:
        x1_ref, proj_ref, act_ref = refs
    else:
        w2_ref, gb_ref, x1_ref, proj_ref, bound_ref, act_ref = refs
    x = x_ref[...]
    h = _rms(x, n1_ref[...]).astype(BF16)
    x1 = x + 0.5 * _swiglu(h, win_ref, wout_ref, act_ref)
    x1_ref[...] = x1
    hm = _rms(x1, nm_ref[...]).astype(BF16)
    if channel_major:
        proj_ref[...] = _dot_nt(wmix_ref[...], hm)
        return
    proj = _dot(hm, wmix_ref[...])
    proj_ref[...] = proj
    soft = lambda z: jnp.maximum(-z, 0.0) + math.log(2.0)
    gk = _dot(proj[:, O_CR:O_CR + C_RP].astype(BF16), w2_ref[...]) + gb_ref[...]
    bound = jnp.maximum(_decay_bound(soft(proj[:, O_AF:O_AF + A_W])), _decay_bound(soft(gk) * (1.0 / C_TAU)))
    bound_ref[...] = jnp.full(bound_ref.shape, bound, F32)


def _out_ffn_kernel(x_ref, *refs, final, channel_major):
    n_y = 3 if channel_major else 1
    y_refs = refs[:n_y]
    wo_ref, n2_ref, win_ref, wout_ref, nf_ref, o_ref, act_ref = refs[n_y:]
    x2 = x_ref[...]
    if channel_major:
        lo = 0
        for y_ref in y_refs:
            w = y_ref.shape[0]
            x2 = x2 + _dot_tn(y_ref[...], wo_ref[lo:lo + w, :])
            lo += w
    else:
        x2 = x2 + _dot(y_refs[0][...], wo_ref[...])
    h = _rms(x2, n2_ref[...]).astype(BF16)
    x3 = x2 + 0.5 * _swiglu(h, win_ref, wout_ref, act_ref)
    if final:
        x3 = _rms(x3, nf_ref[...])
    o_ref[...] = x3


def _resident(shape, layer=None):
    if layer is None:
        return pl.BlockSpec(shape, lambda i: (0,) * len(shape), pipeline_mode=pl.Buffered(1))
    return pl.BlockSpec((None,) + shape, lambda i: (layer,) + (0,) * len(shape), pipeline_mode=pl.Buffered(1))


def _row_tile(tm, width):
    return pl.BlockSpec((tm, width), lambda i: (i, 0))


def _col_tile(tm, height):
    return pl.BlockSpec((height, tm), lambda i: (0, i))


def _ffn_proj(x, n1, win, wout, nm, wmix, layer, gate_params=None):
    m = x.shape[0]
    tm = TM_IN
    channel_major = gate_params is None
    in_specs = [_row_tile(tm, D_MODEL), _resident((1, D_MODEL), layer), _resident((D_MODEL, 2 * D_FF), layer),
                _resident((D_FF, D_MODEL), layer), _resident((1, D_MODEL), layer)]
    if channel_major:
        in_specs += [_resident((NP, D_MODEL), layer)]
        out_specs = [_row_tile(tm, D_MODEL), _col_tile(tm, NP)]
        out_shape = [jax.ShapeDtypeStruct((m, D_MODEL), F32), jax.ShapeDtypeStruct((NP, m), F32)]
        gate_params = ()
    else:
        in_specs += [_resident((D_MODEL, NP), layer), _resident((C_RP, C_QP), layer), _resident((1, C_QP), layer)]
        out_specs = [_row_tile(tm, D_MODEL), _row_tile(tm, NP),
                     pl.BlockSpec((1, SUBLANES, LANES), lambda i: (i, 0, 0))]
        out_shape = [jax.ShapeDtypeStruct((m, D_MODEL), F32), jax.ShapeDtypeStruct((m, NP), F32),
                     jax.ShapeDtypeStruct((m // tm, SUBLANES, LANES), F32)]
    return pl.pallas_call(
        functools.partial(_ffn_proj_kernel, channel_major=channel_major),
        grid=(m // tm,),
        in_specs=in_specs,
        out_specs=out_specs,
        out_shape=out_shape,
        scratch_shapes=[pltpu.VMEM((tm, D_FF), BF16)],
        compiler_params=pltpu.CompilerParams(dimension_semantics=("parallel",), vmem_limit_bytes=VMEM_LIMIT),
        name="ffn1_proj",
    )(x, n1, win, wout, nm, wmix, *gate_params)


def _out_ffn(x, ys, wo, n2, win, wout, nf, layer, channel_major):
    m = x.shape[0]
    tm = TM_OUT
    y_specs = [_col_tile(tm, y.shape[0]) for y in ys] if channel_major else [_row_tile(tm, D_MODEL)]
    return pl.pallas_call(
        functools.partial(_out_ffn_kernel, final=layer == DEPTH - 1, channel_major=channel_major),
        grid=(m // tm,),
        in_specs=[_row_tile(tm, D_MODEL)] + y_specs + [
            _resident((D_MODEL, D_MODEL), layer), _resident((1, D_MODEL), layer),
            _resident((D_MODEL, 2 * D_FF), layer), _resident((D_FF, D_MODEL), layer), _resident((1, D_MODEL))],
        out_specs=_row_tile(tm, D_MODEL),
        out_shape=jax.ShapeDtypeStruct((m, D_MODEL), F32),
        scratch_shapes=[pltpu.VMEM((tm, D_FF), BF16)],
        compiler_params=pltpu.CompilerParams(dimension_semantics=("parallel",), vmem_limit_bytes=VMEM_LIMIT),
        name="outproj_ffn2",
    )(x, *ys, wo, n2, win, wout, nf)


def _log2(n):
    assert n & (n - 1) == 0
    return n.bit_length() - 1


def _head_mask(width, dh, h):
    lane = _iota((1, width), 1)
    return (lane >= dh * h) & (lane < dh * (h + 1))


def _head_grid_mask(rows, cols, dr, dc, n_heads):
    r = _iota((rows, cols), 0)
    c = _iota((rows, cols), 1)
    m = None
    for h in range(n_heads):
        mh = (r >= dr * h) & (r < dr * (h + 1)) & (c >= dc * h) & (c < dc * (h + 1))
        m = mh if m is None else (m | mh)
    return m


def _seg_mean_sq(o, dh, n_heads):
    w = o.shape[1]
    if w > MXU_DIM and MXU_DIM % dh == 0:
        parts = [_seg_mean_sq(o[:, lo:min(lo + MXU_DIM, w)], dh, (min(lo + MXU_DIM, w) - lo) // dh)
                 for lo in range(0, w, MXU_DIM)]
        return jnp.concatenate(parts, axis=1)
    ones = jnp.where(_head_grid_mask(w, w, dh, dh, n_heads), 1.0, 0.0).astype(BF16)
    return _dot((o * o).astype(BF16), ones) * (1.0 / dh)


def _head_norm(o, dh, n_heads):
    return o * lax.rsqrt(_seg_mean_sq(o, dh, n_heads) + EPS)


def _tile_cumsum(g):
    rows = _iota((ROWS, ROWS), 0)
    cols = _iota((ROWS, ROWS), 1)
    return _dot01(jnp.where(cols <= rows, 1.0, 0.0).astype(BF16), g)


def _block_row(b, size, idx):
    c = b.shape[1]
    if size >= SUBLANES:
        b3 = b.reshape(ROWS // size, size, c)
        return jnp.broadcast_to(b3[:, idx:idx + 1, :], b3.shape).reshape(ROWS, c)
    b8 = b.reshape(ROWS // SUBLANES, SUBLANES, c)
    sub = _iota(b8.shape, 1)
    ref = b8[:, idx:idx + 1, :]
    for blk in range(1, SUBLANES // size):
        lo = blk * size
        ref = jnp.where(sub >= lo, b8[:, lo + idx:lo + idx + 1, :], ref)
    return jnp.broadcast_to(ref, b8.shape).reshape(ROWS, c)


def _stack_heads(x, dh, n_heads, extra_mask=None):
    parts = []
    for h in range(n_heads):
        m = _head_mask(x.shape[1], dh, h)
        if extra_mask is not None:
            m = m & extra_mask
        parts.append(jnp.where(m, x, 0.0))
    return jnp.concatenate(parts, axis=0).astype(BF16)


def _apply_scores(p, v, dv, n_heads):
    pcat = jnp.concatenate([p[h].astype(BF16) for h in range(n_heads)], axis=1)
    return _dot(pcat, _stack_heads(v, dv, n_heads))


def _tree_levels(q, k, b, dk, n_heads, first):
    rows = _iota((ROWS, ROWS), 0)
    cols = _iota((ROWS, ROWS), 1)
    rowc = _iota((ROWS, q.shape[1]), 0)
    total = jnp.zeros((n_heads, ROWS, ROWS), F32)
    m = first
    while m < ROWS:
        ref = _block_row(b, 2 * m, m - 1)
        up = (rowc & (2 * m - 1)) >= m
        e = jnp.exp(jnp.where(up, b - ref, ref - b))
        ql = _stack_heads(q * e, dk, n_heads, extra_mask=up)
        kl = jnp.where(up, 0.0, k * e).astype(BF16)
        sc = _dot_nt(ql, kl).reshape(n_heads, ROWS, ROWS)
        sh = _log2(2 * m)
        total = total + jnp.where(((rows >> sh) == (cols >> sh))[None], sc, 0.0)
        m *= 2
    return total


def _block_deviation(b):
    return b - _block_row(b, FACTOR_BLOCK, FACTOR_BLOCK // 2 - 1)


def _gated_intra(q, k, v, b, dev, dk, dv, n_heads, factorise):
    rows = _iota((ROWS, ROWS), 0)
    cols = _iota((ROWS, ROWS), 1)
    if factorise:
        qf = _stack_heads(q * jnp.exp(dev), dk, n_heads)
        kf = (k * jnp.exp(-dev)).astype(BF16)
        sc = _dot_nt(qf, kf).reshape(n_heads, ROWS, ROWS)
        sh = _log2(FACTOR_BLOCK)
        keep = ((rows >> sh) == (cols >> sh)) & (cols <= rows)
        total = jnp.where(keep[None], sc, 0.0) + _tree_levels(q, k, b, dk, n_heads, FACTOR_BLOCK)
    else:
        sc = _dot_nt(_stack_heads(q, dk, n_heads), k.astype(BF16)).reshape(n_heads, ROWS, ROWS)
        total = jnp.where((rows == cols)[None], sc, 0.0) + _tree_levels(q, k, b, dk, n_heads, 1)
    return _apply_scores(total, v, dv, n_heads)


def _column(row_vec):
    c = row_vec.shape[1]
    return jnp.transpose(jnp.broadcast_to(row_vec, (SUBLANES, c)))[:, 0:1]


def _gated_tile(q, k, v, b, dev, dk, dv, n_heads, factorise, s_ref):
    o = _gated_intra(q, k, v, b, dev, dk, dv, n_heads, factorise)
    s0 = s_ref[...]
    o = o + _dot((q * jnp.exp(b)).astype(BF16), s0.astype(BF16))
    b_last = b[ROWS - 1:ROWS, :]
    khat = (k * jnp.exp(b_last - b)).astype(BF16)
    ds = _dot_tn(khat, v.astype(BF16))
    keep = _head_grid_mask(q.shape[1], v.shape[1], dk, dv, n_heads)
    s_ref[...] = s0 * _column(jnp.exp(b_last)) + jnp.where(keep, ds, 0.0)
    return o


_LOG_GAMMA = tuple(math.log(1.0 - 2.0 ** (-5.0 - h)) for h in range(B_HEADS))


def _per_head_lanes(width, dh, values):
    lane = _iota((1, width), 1)
    out = jnp.zeros((1, width), F32)
    for h, val in enumerate(values):
        out = jnp.where((lane >= dh * h) & (lane < dh * (h + 1)), val, out)
    return out


def _rotary(x, cos, sin_signed):
    w = x.shape[1]
    lane = _iota((1, w), 1)
    first_half = (lane & (B_DK - 1)) < (B_DK // 2)
    partner = jnp.where(first_half, pltpu.roll(x, w - B_DK // 2, 1), pltpu.roll(x, B_DK // 2, 1))
    return x * cos + partner * sin_signed


def _lane_groups(width):
    return [(lo, min(lo + MXU_DIM, width)) for lo in range(0, width, MXU_DIM)]


def _retention_tile(q, k, v, s_ref):
    rows = _iota((ROWS, ROWS), 0)
    cols = _iota((ROWS, ROWS), 1)
    causal = cols <= rows
    dist = (rows - cols).astype(F32)
    tau = _iota((ROWS, 1), 0).astype(F32)
    lg = _per_head_lanes(B_W, B_DK, _LOG_GAMMA)
    qhat = (q * jnp.exp(lg * (tau + 1.0))).astype(BF16)
    khat = (k * jnp.exp(lg * (ROWS - 1.0 - tau))).astype(BF16)
    decay = _column(_per_head_lanes(B_W, B_DK, [math.exp(ROWS * g) for g in _LOG_GAMMA]))
    v_bf = v.astype(BF16)
    outs = []
    for lo, hi in _lane_groups(B_W):
        nh = (hi - lo) // B_DK
        sc = _dot_nt(_stack_heads(q[:, lo:hi], B_DK, nh), k[:, lo:hi].astype(BF16)).reshape(nh, ROWS, ROWS)
        gam = jnp.stack([jnp.where(causal, jnp.exp(dist * g), 0.0) for g in _LOG_GAMMA[lo // B_DK:hi // B_DK]])
        s0 = s_ref[lo:hi, lo:hi]
        outs.append(_apply_scores(sc * gam, v[:, lo:hi], B_DV, nh) + _dot(qhat[:, lo:hi], s0.astype(BF16)))
        ds = _dot_tn(khat[:, lo:hi], v_bf[:, lo:hi])
        keep = _head_grid_mask(hi - lo, hi - lo, B_DK, B_DV, nh)
        s_ref[lo:hi, lo:hi] = s0 * decay[lo:hi, :] + jnp.where(keep, ds, 0.0)
    return jnp.concatenate(outs, axis=1)


def _mix_tile(proj_ref, cos_ref, sin_ref, lbl_ref, hn_ref, w2_ref, gb_ref, gn_ref, sa_ref, sb_ref, sc_ref,
              layer, factorise):
    seg = lambda off, w: proj_ref[:, off:off + w]
    log_f, k_a = _hgrn_gate(seg(O_AF, A_W), [lbl_ref[i:i + 1, :] for i in range(DEPTH)], layer)
    b_a = _tile_cumsum(log_f)
    b_c = _tile_cumsum(_gla_log_decay(seg(O_CR, C_RP), w2_ref, gb_ref))
    dev_a = _block_deviation(b_a) if factorise else None
    dev_c = _block_deviation(b_c) if factorise else None
    o_a = _gated_tile(_silu(seg(O_AQ, A_W)), k_a, seg(O_AI, A_W), b_a, dev_a, A_DK, A_DV, A_HEADS, factorise,
                      sa_ref)
    y_a = _head_norm(o_a, A_DV, A_HEADS) * hn_ref[...] * _silu(seg(O_AG, A_W))
    q_b = _rotary(seg(O_BQ, B_W), cos_ref[...], sin_ref[...])
    k_b = _rotary(seg(O_BK, B_W), cos_ref[...], sin_ref[...]) * (B_DK ** -0.5)
    o_b = _retention_tile(q_b, k_b, seg(O_BV, B_W), sb_ref)
    y_b = _head_norm(o_b, B_DV, B_HEADS) * _silu(seg(O_BG, B_W))
    o_c = _gated_tile(seg(O_CQ, C_QP) * (C_DK ** -0.5), seg(O_CK, C_QP), seg(O_CV, C_VW), b_c, dev_c,
                      C_DK, C_DV, C_HEADS, factorise, sc_ref)
    y_c = _head_norm(o_c, C_DV, C_HEADS) * gn_ref[...] * _silu(seg(O_CG, C_VW))
    return jnp.concatenate([y_a, y_b, y_c], axis=1).astype(BF16)


def _mix_prompt_kernel(bound_ref, proj_ref, cos_ref, sin_ref, lbl_ref, hn_ref, w2_ref, gb_ref, gn_ref,
                       y_ref, sa_ref, sb_ref, sc_ref, sa_acc, sb_acc, sc_acc, *, layer, bounds_per_seq,
                       tiles_per_bound):
    t = pl.program_id(1)

    @pl.when(t == 0)
    def _():
        sa_acc[...] = jnp.zeros_like(sa_acc)
        sb_acc[...] = jnp.zeros_like(sb_acc)
        sc_acc[...] = jnp.zeros_like(sc_acc)

    n_par = proj_ref.shape[0]

    def body(factorise):
        for i in range(n_par):
            y_ref[i] = _mix_tile(proj_ref.at[i], cos_ref, sin_ref, lbl_ref, hn_ref, w2_ref, gb_ref, gn_ref,
                                 sa_acc.at[i], sb_acc.at[i], sc_acc.at[i], layer, factorise)

    safe = None
    for i in range(n_par):
        ok = bound_ref[(pl.program_id(0) * n_par + i) * bounds_per_seq + t // tiles_per_bound] < FACTOR_MAX_EXP
        safe = ok if safe is None else jnp.logical_and(safe, ok)

    @pl.when(safe)
    def _():
        body(True)

    @pl.when(jnp.logical_not(safe))
    def _():
        body(False)

    @pl.when(t == pl.num_programs(1) - 1)
    def _():
        for i in range(n_par):
            for h in range(A_HEADS):
                sa_ref[i, h] = sa_acc[i, A_DK * h:A_DK * (h + 1), A_DV * h:A_DV * (h + 1)]
            for h in range(B_HEADS):
                sb_ref[i, h] = sb_acc[i, B_DK * h:B_DK * (h + 1), B_DV * h:B_DV * (h + 1)]
            for h in range(C_HEADS):
                sc_ref[i, h] = sc_acc[i, C_DK * h:C_DK * (h + 1), C_DV * h:C_DV * (h + 1)]


def _const(shape):
    return pl.BlockSpec(shape, lambda *_: (0,) * len(shape))


def _mix_prompt(bound, proj, cos, sin, lbl, hn, w2, gb, gn, layer):
    bsz, seq, _ = proj.shape
    n_par = SEQS_PER_STEP if bsz % SEQS_PER_STEP == 0 else 1
    bounds_per_seq = bound.shape[0] // bsz
    assert (seq // ROWS) % bounds_per_seq == 0
    tile = lambda w: pl.BlockSpec((n_par, ROWS, w), lambda b, t, _: (b, t, 0))
    tab = pl.BlockSpec((ROWS, B_W), lambda b, t, _: (t, 0))
    st = lambda h, dk, dv: pl.BlockSpec((n_par, h, dk, dv), lambda b, t, _: (b, 0, 0, 0))
    return pl.pallas_call(
        functools.partial(_mix_prompt_kernel, layer=layer, bounds_per_seq=bounds_per_seq,
                          tiles_per_bound=(seq // ROWS) // bounds_per_seq),
        grid_spec=pltpu.PrefetchScalarGridSpec(
            num_scalar_prefetch=1,
            grid=(bsz // n_par, seq // ROWS),
            in_specs=[tile(NP), tab, tab, _const((DEPTH, A_W)), _const((1, A_W)), _const((C_RP, C_QP)),
                      _const((1, C_QP)), _const((1, C_VW))],
            out_specs=[tile(D_MODEL), st(A_HEADS, A_DK, A_DV), st(B_HEADS, B_DK, B_DV),
                       st(C_HEADS, C_DK, C_DV)],
            scratch_shapes=[pltpu.VMEM((n_par, A_W, A_W), F32), pltpu.VMEM((n_par, B_W, B_W), F32),
                            pltpu.VMEM((n_par, C_QP, C_VW), F32)]),
        out_shape=[jax.ShapeDtypeStruct((bsz, seq, D_MODEL), BF16),
                   jax.ShapeDtypeStruct((bsz, A_HEADS, A_DK, A_DV), F32),
                   jax.ShapeDtypeStruct((bsz, B_HEADS, B_DK, B_DV), F32),
                   jax.ShapeDtypeStruct((bsz, C_HEADS, C_DK, C_DV), F32)],
        compiler_params=pltpu.CompilerParams(dimension_semantics=("parallel", "arbitrary"),
                                             vmem_limit_bytes=VMEM_LIMIT),
        name="mix_prompt",
    )(bound, proj, cos, sin, lbl, hn, w2, gb, gn)


def _head_recurrence(q, k, v, logd, gate, norm_col, s_ref, snew_ref, y_ref, qb_ref, kb_ref, eb_ref):
    n_t = len(q)
    dk, n_seq = q[0].shape
    dv = v[0].shape[0]
    b = [logd[0]]
    for t in range(1, n_t):
        b.append(b[-1] + logd[t])
    o = []
    for t in range(n_t):
        acc = jnp.sum(q[t] * k[t], axis=0, keepdims=True) * v[t]
        for s in range(t):
            sc = jnp.sum(q[t] * k[s] * jnp.exp(b[t] - b[s]), axis=0, keepdims=True)
            acc = acc + sc * v[s]
        o.append(acc)
    rep = lambda x: jnp.broadcast_to(x[:, None, :], (dk, SUBLANES, n_seq))
    for t in range(n_t):
        qb_ref[t] = rep(q[t] * jnp.exp(b[t]))
        kb_ref[t] = rep(k[t] * jnp.exp(b[n_t - 1] - b[t]))
    eb_ref[...] = rep(jnp.exp(b[n_t - 1]))
    chunks = [[] for _ in range(n_t)]
    for vc in range(dv // SUBLANES):
        rows = pl.ds(vc * SUBLANES, SUBLANES)
        v_rows = [v[t][vc * SUBLANES:(vc + 1) * SUBLANES, :] for t in range(n_t)]

        def body(d, carry):
            s_d = s_ref[d, rows, :]
            terms = [eb_ref[d] * s_d] + [kb_ref[t, d] * v_rows[t] for t in range(n_t)]
            while len(terms) > 1:
                terms = [a + c for a, c in zip(terms[::2], terms[1::2])] + terms[len(terms) & ~1:]
            snew_ref[d, rows, :] = terms[0]
            return tuple(carry[t] + qb_ref[t, d] * s_d for t in range(n_t))

        inter = lax.fori_loop(0, dk, body, tuple(jnp.zeros((SUBLANES, n_seq), F32) for _ in range(n_t)),
                              unroll=STATE_ROW_UNROLL)
        for t in range(n_t):
            chunks[t].append(inter[t])
    for t in range(n_t):
        o_t = o[t] + jnp.concatenate(chunks[t], axis=0)
        ms = jnp.mean(o_t * o_t, axis=0, keepdims=True)
        y_t = o_t * lax.rsqrt(ms + EPS) * _silu(gate[t])
        if norm_col is not None:
            y_t = y_t * norm_col
        y_ref[:, t * n_seq:(t + 1) * n_seq] = y_t.astype(BF16)


def _state_slots(rest):
    snew_ref, y_ref, qb_ref, kb_ref, eb_ref = rest[-5:]
    done = snew_ref.shape[0] - 1
    if done:
        snew_ref[0:done] = rest[0][...]
    return snew_ref.at[done], y_ref, qb_ref, kb_ref, eb_ref


def _slabs(x, n_t):
    n_seq = x.shape[1] // n_t
    return [x[:, t * n_seq:(t + 1) * n_seq] for t in range(n_t)]


def _head_rows(ref, dh):
    h = pl.program_id(0)
    return ref[pl.ds(pl.multiple_of(h * dh, 2 * SUBLANES), dh), :]


def _hgrn_head_kernel(q_ref, f_ref, i_ref, g_ref, lbl_ref, hn_ref, s_ref, *rest, layer, n_t):
    snew_ref, y_ref, qb_ref, kb_ref, eb_ref = _state_slots(rest)
    log_f, k_a = _hgrn_gate(f_ref[...], [lbl_ref[i] for i in range(DEPTH)], layer)
    _head_recurrence(_slabs(_silu(q_ref[...]), n_t), _slabs(k_a, n_t), _slabs(i_ref[...], n_t),
                     _slabs(log_f, n_t), _slabs(g_ref[...], n_t), hn_ref[...], s_ref, snew_ref, y_ref,
                     qb_ref, kb_ref, eb_ref)


def _ret_head_kernel(q_ref, k_ref, v_ref, g_ref, cos_ref, sin_ref, lg_ref, s_ref, *rest, n_t):
    snew_ref, y_ref, qb_ref, kb_ref, eb_ref = _state_slots(rest)
    half = B_DK // 2

    def rot(x):
        x1, x2 = x[:half], x[half:]
        c, s = cos_ref[...], sin_ref[...]
        return jnp.concatenate([x1 * c - x2 * s, x1 * s + x2 * c], axis=0)

    q = rot(q_ref[...])
    k = rot(k_ref[...]) * (B_DK ** -0.5)
    logd = jnp.broadcast_to(lg_ref[...], (B_DK, q.shape[1] // n_t))
    _head_recurrence(_slabs(q, n_t), _slabs(k, n_t), _slabs(v_ref[...], n_t), [logd] * n_t,
                     _slabs(g_ref[...], n_t), None, s_ref, snew_ref, y_ref, qb_ref, kb_ref, eb_ref)


def _gla_head_kernel(q_ref, k_ref, v_ref, g_ref, r_ref, w2t_ref, gb_ref, gn_ref, s_ref, *rest, n_t):
    snew_ref, y_ref, qb_ref, kb_ref, eb_ref = _state_slots(rest)
    gk = _dot(_head_rows(w2t_ref, C_DK), r_ref[...].astype(BF16)) + _head_rows(gb_ref, C_DK)
    log_a = jax.nn.log_sigmoid(gk) * (1.0 / C_TAU)
    _head_recurrence(_slabs(_head_rows(q_ref, C_DK) * (C_DK ** -0.5), n_t), _slabs(_head_rows(k_ref, C_DK), n_t),
                     _slabs(_head_rows(v_ref, C_DV), n_t), _slabs(log_a, n_t),
                     _slabs(_head_rows(g_ref, C_DV), n_t), gn_ref[...], s_ref, snew_ref, y_ref,
                     qb_ref, kb_ref, eb_ref)


def _head_call(body, name, projt, segs, extras, extra_specs, state, prev, layer, n_heads, dk, dv, n_t):
    m = projt.shape[1]
    n_seq = m // n_t
    seg_specs = []
    for off, rows, per_head in segs:
        assert off % rows == 0
        if per_head:
            seg_specs.append(pl.BlockSpec((rows, m), functools.partial(lambda i, h: (i + h, 0), off // rows)))
        else:
            seg_specs.append(pl.BlockSpec((rows, m), functools.partial(lambda i, h: (i, 0), off // rows)))
    layers = lambda n: pl.BlockSpec((n, None, dk, dv, n_seq), lambda h: (0, h, 0, 0, 0))
    in_specs = seg_specs + extra_specs + [pl.BlockSpec((None, None, dk, dv, n_seq), lambda h: (layer, h, 0, 0, 0))]
    operands = [projt] * len(segs) + extras + [state]
    if prev is not None:
        in_specs.append(layers(layer))
        operands.append(prev)
    rep_rows = pltpu.VMEM((n_t, dk, SUBLANES, n_seq), F32)
    return pl.pallas_call(
        body,
        grid=(n_heads,),
        in_specs=in_specs,
        out_specs=[layers(layer + 1), pl.BlockSpec((dv, m), lambda h: (h, 0))],
        out_shape=[jax.ShapeDtypeStruct((layer + 1,) + state.shape[1:], F32),
                   jax.ShapeDtypeStruct((n_heads * dv, m), BF16)],
        scratch_shapes=[rep_rows, rep_rows, pltpu.VMEM((dk, SUBLANES, n_seq), F32)],
        compiler_params=pltpu.CompilerParams(dimension_semantics=("parallel",), vmem_limit_bytes=VMEM_LIMIT),
        name=name,
    )(*operands)


def _rope_angles(pos):
    half = B_DK // 2
    inv_freq = ROPE_BASE ** (-jnp.arange(half, dtype=F32) / half)
    return pos[:, None] * inv_freq[None, :]


def _rope_tables(pos):
    ang = _rope_angles(pos)
    cos, sin = jnp.cos(ang), jnp.sin(ang)
    cos_h = jnp.concatenate([cos, cos], axis=1)
    sin_h = jnp.concatenate([-sin, sin], axis=1)
    return jnp.tile(cos_h, (1, B_HEADS)), jnp.tile(sin_h, (1, B_HEADS))


def _rope_tables_channel_major(pos, n_seq):
    ang = _rope_angles(pos)
    rep = lambda a: jnp.repeat(a.T, n_seq, axis=1)
    return rep(jnp.cos(ang)), rep(jnp.sin(ang))


def _pad_w_in(w):
    z = lambda n: jnp.zeros(w.shape[:2] + (n,), w.dtype)
    c0 = 2560
    return jnp.concatenate([
        w[..., :c0],
        w[..., c0:c0 + C_QW], z(C_QP - C_QW),
        w[..., c0 + C_QW:c0 + 2 * C_QW], z(C_QP - C_QW),
        w[..., c0 + 2 * C_QW:c0 + 2 * C_QW + 2 * C_VW],
        w[..., c0 + 2 * C_QW + 2 * C_VW:], z(C_RP - C_RANK)], axis=-1)


def kernel(x_prompt, x_sample, state_hgrn, state_ret, state_gla, ffn1_norm, ffn1_w_in, ffn1_w_out, mix_norm, w_in, hgrn_lb_logits, hgrn_norm, gla_w2, gla_b, gla_norm, w_out, ffn2_norm, ffn2_w_in, ffn2_w_out, final_norm):
    bsz, seq, _ = x_prompt.shape
    dbs, dseq, _ = x_sample.shape
    assert seq % ROWS == 0 and seq % TM_IN == 0 and dbs % LANES == 0
    assert all(n % tm == 0 for n in (bsz * seq, dbs * dseq) for tm in (TM_IN, TM_OUT))

    f1_in, f1_out = ffn1_w_in.astype(BF16), ffn1_w_out.astype(BF16)
    f2_in, f2_out = ffn2_w_in.astype(BF16), ffn2_w_out.astype(BF16)
    wo = w_out.astype(BF16)
    wmix = _pad_w_in(w_in).astype(BF16)
    stacked = lambda a: a.reshape(DEPTH, 1, -1).astype(F32)
    n1, nm, n2 = stacked(ffn1_norm), stacked(mix_norm), stacked(ffn2_norm)
    nf = final_norm.reshape(1, -1).astype(F32)
    lbl = hgrn_lb_logits.astype(F32)
    w2_pad = jnp.zeros((DEPTH, C_RP, C_QP), BF16).at[:, :C_RANK, :C_QW].set(gla_w2.astype(BF16))
    gb_pad = jnp.zeros((DEPTH, 1, C_QP), F32).at[:, 0, :C_QW].set(gla_b)

    cos_p, sin_p = _rope_tables(jnp.arange(seq, dtype=F32))
    x = x_prompt.reshape(bsz * seq, D_MODEL)
    p_states = []
    for l in range(DEPTH):
        x1, proj, bound = _ffn_proj(x, n1, f1_in, f1_out, nm, wmix, l, (w2_pad, gb_pad))
        y, sa, sb, sc = _mix_prompt(bound[:, 0, 0], proj.reshape(bsz, seq, NP), cos_p, sin_p, lbl,
                                    jnp.tile(hgrn_norm[l], A_HEADS).reshape(1, A_W), w2_pad[l], gb_pad[l],
                                    jnp.tile(gla_norm[l], C_HEADS).reshape(1, C_VW), l)
        x = _out_ffn(x1, [y.reshape(bsz * seq, D_MODEL)], wo, n2, f2_in, f2_out, nf, l, False)
        p_states.append((sa, sb, sc))
    y_prompt = x.reshape(bsz, seq, D_MODEL)

    m = dbs * dseq
    x = jnp.transpose(x_sample, (1, 0, 2)).reshape(m, D_MODEL)
    states = [jnp.transpose(s, (0, 2, 3, 4, 1)) for s in (state_hgrn, state_ret, state_gla)]
    new = [None, None, None]
    wmix_t = jnp.swapaxes(wmix, 1, 2)
    cos_s, sin_s = _rope_tables_channel_major(PAST_LEN + jnp.arange(dseq, dtype=F32), dbs)
    lg = jnp.asarray(_LOG_GAMMA, F32).reshape(B_HEADS, 1, 1)
    lbl_col = lbl.reshape(DEPTH, A_W, 1)
    w2_t = jnp.swapaxes(w2_pad, 1, 2)
    gb_col = jnp.swapaxes(gb_pad, 1, 2)
    whole = lambda shape: pl.BlockSpec(shape, lambda h: (0,) * len(shape))
    per_head = lambda off, rows: (off, rows, True)
    segment = lambda off, rows: (off, rows, False)
    for l in range(DEPTH):
        x1, projt = _ffn_proj(x, n1, f1_in, f1_out, nm, wmix_t, l)
        new[0], ya = _head_call(
            functools.partial(_hgrn_head_kernel, layer=l, n_t=dseq), "mix_sample_hgrn", projt,
            [per_head(O_AQ, A_DK), per_head(O_AF, A_DK), per_head(O_AI, A_DV), per_head(O_AG, A_DV)],
            [lbl_col, hgrn_norm[l].reshape(A_DV, 1)],
            [pl.BlockSpec((DEPTH, A_DK, 1), lambda h: (0, h, 0)), whole((A_DV, 1))],
            states[0], new[0], l, A_HEADS, A_DK, A_DV, dseq)
        new[1], yb = _head_call(
            functools.partial(_ret_head_kernel, n_t=dseq), "mix_sample_ret", projt,
            [per_head(O_BQ, B_DK), per_head(O_BK, B_DK), per_head(O_BV, B_DV), per_head(O_BG, B_DV)],
            [cos_s, sin_s, lg],
            [whole((B_DK // 2, m)), whole((B_DK // 2, m)), pl.BlockSpec((None, 1, 1), lambda h: (h, 0, 0))],
            states[1], new[1], l, B_HEADS, B_DK, B_DV, dseq)
        new[2], yc = _head_call(
            functools.partial(_gla_head_kernel, n_t=dseq), "mix_sample_gla", projt,
            [segment(O_CQ, C_QP), segment(O_CK, C_QP), segment(O_CV, C_VW), segment(O_CG, C_VW),
             segment(O_CR, C_RP)],
            [w2_t[l], gb_col[l], gla_norm[l].reshape(C_DV, 1)],
            [whole((C_QP, C_RP)), whole((C_QP, 1)), whole((C_DV, 1))],
            states[2], new[2], l, C_HEADS, C_DK, C_DV, dseq)
        x = _out_ffn(x1, [ya, yb, yc], wo, n2, f2_in, f2_out, nf, l, True)
    y_sample = jnp.transpose(x.reshape(dseq, dbs, D_MODEL), (1, 0, 2))
    s_states = [jnp.transpose(s, (0, 4, 1, 2, 3)) for s in new]

    stack = lambda i: jnp.stack([s[i] for s in p_states])
    return (y_prompt, y_sample, stack(0), stack(1), stack(2), s_states[0], s_states[1], s_states[2])
```
